```python
import jax, jax.numpy as jnp
from jax import lax
import numpy as np

D_MODEL = 2048
BATCH = 4
SEQ = 2048
DEPTH = 4
DEC_BATCH = 128
DEC_SEQ = 8
PAST_LEN = 16384
PAGE_SIZE = 128

D_MIX = D_MODEL
HEAD_DIM = 128
A_WIDTH = D_MIX // 2
A_HEADS = A_WIDTH // HEAD_DIM
B_WIDTH = D_MIX - A_WIDTH
B_GROUPS = B_WIDTH // HEAD_DIM
CHUNK = 128
CONV_W = 3
D_FF = ((8 * D_MODEL) // 3 + 255) // 256 * 256
IN_COLS = 2 * A_WIDTH + 3 * B_WIDTH
EPS = 1e-6

kernel_name = "hybrid_chunk_sgu_shortconv_convffn_step"


def rmsnorm(x, g):
    xf = x.astype(jnp.float32)
    xf = xf * lax.rsqrt(jnp.mean(xf * xf, axis=-1, keepdims=True) + EPS)
    return (xf * g.astype(jnp.float32)).astype(x.dtype)


def group_rmsnorm(y, g, n_groups):
    shp = y.shape
    yf = y.astype(jnp.float32).reshape(shp[:-1] + (n_groups, shp[-1] // n_groups))
    yf = yf * lax.rsqrt(jnp.mean(yf * yf, axis=-1, keepdims=True) + EPS)
    return (yf.reshape(shp) * g.astype(jnp.float32)).astype(y.dtype)


def causal_dwconv(z, prev, w, b):
    T = z.shape[1]
    zc = jnp.concatenate([prev.astype(z.dtype), z], axis=1)
    out = b.astype(z.dtype) + sum(w[k].astype(z.dtype) * zc[:, k:k + T] for k in range(CONV_W))
    return out, zc[:, T:]


def chunk_sgu(v, w_s, b_s):
    Bn, T, _ = v.shape
    L = min(T, CHUNK)
    n = -(-T // CHUNK)
    pad = n * L - T
    vp = jnp.pad(v, ((0, 0), (0, pad), (0, 0)))
    vr = vp.reshape(Bn, n, L, A_HEADS, HEAD_DIM)
    mask = jnp.tril(jnp.ones((L, L), dtype=bool))
    ws = w_s[:, :L, :L]
    ws = jnp.where(mask[None], ws, jnp.zeros_like(ws)).astype(v.dtype)
    bias = b_s[:, :L].T.astype(v.dtype)[None, None, :, :, None]
    mixed = jnp.einsum('hij,bcjhd->bcihd', ws, vr) + bias
    return mixed.reshape(Bn, n * L, A_WIDTH)[:, :T]


def layer(x, mix_prev, ffn_prev, norm_mix_g, w_in, sgu_norm_g, w_s, b_s,
          conv_mix_w, conv_mix_b, out_norm_a_g, out_norm_b_g, w_out,
          norm_ffn_g, w_gate, w_up, conv_ffn_w, conv_ffn_b, w_down):
    h = rmsnorm(x, norm_mix_g)
    p = h @ w_in
    o1, o2, o3, o4 = A_WIDTH, 2 * A_WIDTH, 2 * A_WIDTH + B_WIDTH, 2 * A_WIDTH + 2 * B_WIDTH
    u = jax.nn.gelu(p[..., :o1])
    v = rmsnorm(jax.nn.gelu(p[..., o1:o2]), sgu_norm_g)
    bg, cg, xb = p[..., o2:o3], p[..., o3:o4], p[..., o4:]
    y_a = u * chunk_sgu(v, w_s, b_s)
    zc, new_mix = causal_dwconv(cg * xb, mix_prev, conv_mix_w, conv_mix_b)
    y_b = bg * zc
    y = jnp.concatenate([group_rmsnorm(y_a, out_norm_a_g, A_HEADS),
                         group_rmsnorm(y_b, out_norm_b_g, B_GROUPS)], axis=-1)
    x = x + y @ w_out
    h = rmsnorm(x, norm_ffn_g)
    gate, new_ffn = causal_dwconv(h @ w_gate, ffn_prev, conv_ffn_w, conv_ffn_b)
    x = x + (jax.nn.silu(gate) * (h @ w_up)) @ w_down
    return x, new_mix, new_ffn, v


def setup_inputs(seed: int = 0) -> dict:
    key = jax.random.key(seed)
    ks = jax.random.split(key, 24)
    f32 = jnp.float32
    nrm = lambda k, s, sc=1.0: (jax.random.normal(k, s, f32) * sc)
    gain = lambda k, s: 1.0 + 0.1 * jax.random.normal(k, s, f32)
    return {
        "x_prompt": nrm(ks[0], (BATCH, SEQ, D_MODEL)),
        "x_sample": nrm(ks[1], (DEC_BATCH, DEC_SEQ, D_MODEL)),
        "state_conv_mix": nrm(ks[2], (DEPTH, DEC_BATCH, CONV_W - 1, B_WIDTH)),
        "state_conv_ffn": nrm(ks[3], (DEPTH, DEC_BATCH, CONV_W - 1, D_FF)),
        "norm_mix_g": gain(ks[4], (DEPTH, D_MODEL)),
        "w_in": nrm(ks[5], (DEPTH, D_MODEL, IN_COLS), D_MODEL ** -0.5),
        "sgu_norm_g": gain(ks[6], (DEPTH, A_WIDTH)),
        "w_s": nrm(ks[7], (DEPTH, A_HEADS, CHUNK, CHUNK), CHUNK ** -0.5),
        "b_s": gain(ks[8], (DEPTH, A_HEADS, CHUNK)),
        "conv_mix_w": nrm(ks[9], (DEPTH, CONV_W, B_WIDTH), CONV_W ** -0.5),
        "conv_mix_b": nrm(ks[10], (DEPTH, B_WIDTH), 0.01),
        "out_norm_a_g": gain(ks[11], (DEPTH, A_WIDTH)),
        "out_norm_b_g": gain(ks[12], (DEPTH, B_WIDTH)),
        "w_out": nrm(ks[13], (DEPTH, D_MIX, D_MODEL), D_MIX ** -0.5),
        "norm_ffn_g": gain(ks[14], (DEPTH, D_MODEL)),
        "w_gate": nrm(ks[15], (DEPTH, D_MODEL, D_FF), D_MODEL ** -0.5),
        "w_up": nrm(ks[16], (DEPTH, D_MODEL, D_FF), D_MODEL ** -0.5),
        "conv_ffn_w": nrm(ks[17], (DEPTH, CONV_W, D_FF), CONV_W ** -0.5),
        "conv_ffn_b": nrm(ks[18], (DEPTH, D_FF), 0.01),
        "w_down": nrm(ks[19], (DEPTH, D_FF, D_MODEL), D_FF ** -0.5),
        "final_norm_g": gain(ks[20], (D_MODEL,)),
    }


def reference(x_prompt, x_sample, state_conv_mix, state_conv_ffn,
              norm_mix_g, w_in, sgu_norm_g, w_s, b_s, conv_mix_w, conv_mix_b,
              out_norm_a_g, out_norm_b_g, w_out, norm_ffn_g, w_gate, w_up,
              conv_ffn_w, conv_ffn_b, w_down, final_norm_g):
    xp, xs = x_prompt, x_sample
    mix_p, ffn_p, mix_s, ffn_s, v_s = [], [], [], [], []
    for l in range(DEPTH):
        params = (norm_mix_g[l], w_in[l], sgu_norm_g[l], w_s[l], b_s[l],
                  conv_mix_w[l], conv_mix_b[l], out_norm_a_g[l], out_norm_b_g[l], w_out[l],
                  norm_ffn_g[l], w_gate[l], w_up[l], conv_ffn_w[l], conv_ffn_b[l], w_down[l])
        zero_mix = jnp.zeros((xp.shape[0], CONV_W - 1, B_WIDTH), xp.dtype)
        zero_ffn = jnp.zeros((xp.shape[0], CONV_W - 1, D_FF), xp.dtype)
        xp, nm, nf, _ = layer(xp, zero_mix, zero_ffn, *params)
        mix_p.append(nm)
        ffn_p.append(nf)
        xs, nm, nf, vrows = layer(xs, state_conv_mix[l], state_conv_ffn[l], *params)
        mix_s.append(nm)
        ffn_s.append(nf)
        v_s.append(vrows)
    y_prompt = rmsnorm(xp, final_norm_g)
    y_sample = rmsnorm(xs, final_norm_g)
    new_conv_mix_prompt = jnp.stack(mix_p)
    new_conv_ffn_prompt = jnp.stack(ffn_p)
    new_conv_mix_sample = jnp.stack(mix_s)
    new_conv_ffn_sample = jnp.stack(ffn_s)
    new_chunk_v_sample = jnp.stack(v_s)
    return (y_prompt, y_sample, new_conv_mix_prompt, new_conv_ffn_prompt,
            new_conv_mix_sample, new_conv_ffn_sample, new_chunk_v_sample)
```

```python
import functools

import jax
import jax.numpy as jnp
from jax import lax
from jax.experimental import pallas as pl
from jax.experimental.pallas import tpu as pltpu

D_MODEL = 2048
A_WIDTH = 1024
B_WIDTH = 1024
HEAD_DIM = 128
N_HEADS = 8
CHUNK = 128
CONV_W = 3
D_FF = 5632
IN_COLS = 2 * A_WIDTH + 3 * B_WIDTH
EPS = 1e-6

VMEM_LIMIT_BYTES = 56 * 1024 * 1024

BM_IN = 256
BS_IN = BM_IN // 8
BM_OUT = 256
BM_FFN = 1024
TF = 512
TN = 512
BM_NORM = 512

F32 = jnp.float32
BF16 = jnp.bfloat16


def _rms(x, g):
    ms = jnp.mean(x * x, axis=-1, keepdims=True)
    return x * lax.rsqrt(ms + EPS) * g


def _gelu(x):
    c = 0.7978845608028654
    return 0.5 * x * (1.0 + jnp.tanh(c * (x + 0.044715 * (x * x * x))))


def _group_norm(y, g):
    outs = []
    for hd in range(N_HEADS):
        blk = y[:, hd * HEAD_DIM:(hd + 1) * HEAD_DIM]
        ms = jnp.mean(blk * blk, axis=-1, keepdims=True)
        outs.append(blk * lax.rsqrt(ms + EPS))
    return jnp.concatenate(outs, axis=-1) * g


def _shift_rows_prompt(z, carry):
    r1 = pltpu.roll(z, 1, 0)
    r2 = pltpu.roll(z, 2, 0)
    c1 = pltpu.roll(carry, 1, 0)
    c2 = pltpu.roll(carry, 2, 0)
    rid = lax.broadcasted_iota(jnp.int32, carry.shape, 0)
    top1 = jnp.where(rid < 1, c1, r1[0:8])
    top2 = jnp.where(rid < 2, c2, r2[0:8])
    p1 = jnp.concatenate([top1, r1[8:]], axis=0)
    p2 = jnp.concatenate([top2, r2[8:]], axis=0)
    return p1, p2


def _shift_rows_sample(z, st, bs):
    p1 = jnp.concatenate([st[1], z[:7 * bs]], axis=0)
    p2 = jnp.concatenate([st[0], st[1], z[:6 * bs]], axis=0)
    return p1, p2


def _conv3(z, p1, p2, cw, cb):
    return cb + cw[0:1] * p2 + cw[1:2] * p1 + cw[2:3] * z


def _in_mix_kernel(*refs, sample, has_xb, tiles_per_seq):
    it = iter(refs)
    xa_ref = next(it)
    xb_ref = next(it) if has_xb else None
    g1_ref, w_in_ref, gv_ref = next(it), next(it), next(it)
    if sample:
        w8_ref, b8_ref, st_in_ref = next(it), next(it), next(it)
    else:
        ws_ref, bs_ref = next(it), next(it)
    cw_ref, cb_ref, ga_ref, gb_ref = next(it), next(it), next(it), next(it)
    y_ref, st_out_ref = next(it), next(it)
    if sample:
        v_out_ref = next(it)
    else:
        carry_ref = next(it)

    bm = BM_IN
    x = xa_ref[...]
    if has_xb:
        x = x + xb_ref[...]
    x = x.reshape(bm, D_MODEL)
    h = _rms(x, g1_ref[...]).astype(BF16)

    def proj(c0):
        return jnp.dot(h, w_in_ref[:, c0:c0 + 1024], preferred_element_type=F32)

    v = _rms(_gelu(proj(A_WIDTH)), gv_ref[...])
    u = _gelu(proj(0))

    if sample:
        v_out_ref[...] = v.reshape(8, BS_IN, A_WIDTH)
        heads = []
        for hd in range(N_HEADS):
            vh = [v[t * BS_IN:(t + 1) * BS_IN, hd * HEAD_DIM:(hd + 1) * HEAD_DIM] for t in range(8)]
            rows = []
            for i in range(8):
                acc = w8_ref[(hd * 8 + i) * 8] * vh[0]
                for j in range(1, i + 1):
                    acc = acc + w8_ref[(hd * 8 + i) * 8 + j] * vh[j]
                rows.append(acc + b8_ref[hd * 8 + i])
            heads.append(jnp.concatenate(rows, axis=0))
        mixed = jnp.concatenate(heads, axis=1)
    else:
        nch = bm // CHUNK
        vb = v.astype(BF16)
        row = lax.broadcasted_iota(jnp.int32, (CHUNK, CHUNK), 0)
        col = lax.broadcasted_iota(jnp.int32, (CHUNK, CHUNK), 1)
        heads = []
        for hd in range(N_HEADS):
            wm = jnp.where(row >= col, ws_ref[hd], 0.0).astype(BF16)
            vh = jnp.concatenate(
                [vb[c * CHUNK:(c + 1) * CHUNK, hd * HEAD_DIM:(hd + 1) * HEAD_DIM] for c in range(nch)], axis=1)
            mh = jnp.dot(wm, vh, preferred_element_type=F32)
            bias = bs_ref[hd]
            heads.append(jnp.concatenate(
                [mh[:, c * HEAD_DIM:(c + 1) * HEAD_DIM] + bias for c in range(nch)], axis=0))
        mixed = jnp.concatenate(heads, axis=1)
    y_a = _group_norm(u * mixed, ga_ref[...]).astype(BF16)
    if sample:
        y_ref[:, :, 0:A_WIDTH] = y_a.reshape(8, BS_IN, A_WIDTH)
    else:
        y_ref[:, 0:A_WIDTH] = y_a

    z = proj(2 * A_WIDTH + B_WIDTH) * proj(2 * A_WIDTH + 2 * B_WIDTH)
    if sample:
        p1, p2 = _shift_rows_sample(z, st_in_ref[...], BS_IN)
        st_out_ref[...] = z[6 * BS_IN:].reshape(2, BS_IN, B_WIDTH)
    else:
        first = (pl.program_id(0) % tiles_per_seq) == 0
        carry = jnp.where(first, 0.0, carry_ref[...])
        p1, p2 = _shift_rows_prompt(z, carry)
        carry_ref[...] = z[bm - 8:bm]
        st_out_ref[...] = z[bm - 2:bm]
    zc = _conv3(z, p1, p2, cw_ref[...], cb_ref[...])
    y_b = _group_norm(proj(2 * A_WIDTH) * zc, gb_ref[...]).astype(BF16)
    if sample:
        y_ref[:, :, A_WIDTH:] = y_b.reshape(8, BS_IN, B_WIDTH)
    else:
        y_ref[:, A_WIDTH:] = y_b


def _row_spec(l, n):
    return pl.BlockSpec((None, 1, n), lambda i, l=l: (l, 0, 0))


def _in_mix(xa, xb, l, p, *, sample, st_in=None):
    has_xb = xb is not None
    rows = 8 * 128 if sample else xa.shape[0]
    n_tiles = rows // BM_IN
    if sample:
        x_spec = pl.BlockSpec((8, BS_IN, D_MODEL), lambda i: (0, i, 0))
    else:
        x_spec = pl.BlockSpec((BM_IN, D_MODEL), lambda i: (i, 0))
    args = [xa] + ([xb] if has_xb else [])
    specs = [x_spec] * len(args)
    args += [p["norm_mix_g"], p["w_in"], p["sgu_norm_g"]]
    specs += [_row_spec(l, D_MODEL),
              pl.BlockSpec((None, D_MODEL, IN_COLS), lambda i: (l, 0, 0), pipeline_mode=pl.Buffered(1)),
              _row_spec(l, A_WIDTH)]
    if sample:
        args += [p["w8"][l], p["b8"][l], st_in]
        specs += [pl.BlockSpec(memory_space=pltpu.SMEM), pl.BlockSpec(memory_space=pltpu.SMEM),
                  pl.BlockSpec((None, 2, BS_IN, B_WIDTH), lambda i: (l, 0, i, 0))]
    else:
        args += [p["w_s"], p["b_s_full"]]
        specs += [pl.BlockSpec((None, N_HEADS, CHUNK, CHUNK), lambda i: (l, 0, 0, 0)),
                  pl.BlockSpec((None, N_HEADS, CHUNK, HEAD_DIM), lambda i: (l, 0, 0, 0))]
    args += [p["conv_mix_w"], p["conv_mix_b"], p["out_norm_a_g"], p["out_norm_b_g"]]
    specs += [pl.BlockSpec((None, CONV_W, B_WIDTH), lambda i: (l, 0, 0)),
              _row_spec(l, B_WIDTH), _row_spec(l, A_WIDTH), _row_spec(l, B_WIDTH)]

    scratch = []
    if sample:
        out_shape = [jax.ShapeDtypeStruct((8, 128, D_MODEL), BF16)]
        out_specs = [pl.BlockSpec((8, BS_IN, D_MODEL), lambda i: (0, i, 0))]
        out_shape += [jax.ShapeDtypeStruct((2, 128, B_WIDTH), F32),
                      jax.ShapeDtypeStruct((8, 128, A_WIDTH), F32)]
        out_specs += [pl.BlockSpec((2, BS_IN, B_WIDTH), lambda i: (0, i, 0)),
                      pl.BlockSpec((8, BS_IN, A_WIDTH), lambda i: (0, i, 0))]
        tiles_per_seq = 1
    else:
        tiles_per_seq = 2048 // BM_IN
        n_seq = rows // 2048
        out_shape = [jax.ShapeDtypeStruct((rows, D_MODEL), BF16)]
        out_specs = [pl.BlockSpec((BM_IN, D_MODEL), lambda i: (i, 0))]
        out_shape += [jax.ShapeDtypeStruct((n_seq, 2, B_WIDTH), F32)]
        out_specs += [pl.BlockSpec((None, 2, B_WIDTH), lambda i: (i // tiles_per_seq, 0, 0))]
        scratch = [pltpu.VMEM((8, B_WIDTH), F32)]

    kern = functools.partial(_in_mix_kernel, sample=sample, has_xb=has_xb, tiles_per_seq=tiles_per_seq)
    return pl.pallas_call(
        kern,
        grid=(n_tiles,),
        in_specs=specs,
        out_specs=out_specs,
        out_shape=out_shape,
        scratch_shapes=scratch,
        compiler_params=pltpu.CompilerParams(
            dimension_semantics=("arbitrary",), vmem_limit_bytes=VMEM_LIMIT_BYTES),
        name="in_mix_sample" if sample else "in_mix_prompt",
    )(*args)


def _out_proj_kernel(*refs, has_xb):
    it = iter(refs)
    xa_ref = next(it)
    xb_ref = next(it) if has_xb else None
    y_ref, w_ref, g_ref, x1_ref, h2_ref = next(it), next(it), next(it), next(it), next(it)
    x = xa_ref[...]
    if has_xb:
        x = x + xb_ref[...]
    x1 = x + jnp.dot(y_ref[...], w_ref[...], preferred_element_type=F32)
    x1_ref[...] = x1
    h2_ref[...] = _rms(x1, g_ref[...]).astype(BF16)


def _out_proj(xa, xb, y, l, p):
    has_xb = xb is not None
    rows = xa.shape[0]
    x_spec = pl.BlockSpec((BM_OUT, D_MODEL), lambda i: (i, 0))
    args = [xa] + ([xb] if has_xb else []) + [y, p["w_out"], p["norm_ffn_g"]]
    specs = [x_spec] * (2 if has_xb else 1) + [
        x_spec,
        pl.BlockSpec((None, D_MODEL, D_MODEL), lambda i: (l, 0, 0), pipeline_mode=pl.Buffered(1)),
        _row_spec(l, D_MODEL)]
    return pl.pallas_call(
        functools.partial(_out_proj_kernel, has_xb=has_xb),
        grid=(rows // BM_OUT,),
        in_specs=specs,
        out_specs=[x_spec, x_spec],
        out_shape=[jax.ShapeDtypeStruct((rows, D_MODEL), F32),
                   jax.ShapeDtypeStruct((rows, D_MODEL), BF16)],
        compiler_params=pltpu.CompilerParams(
            dimension_semantics=("arbitrary",), vmem_limit_bytes=VMEM_LIMIT_BYTES),
        name="out_proj",
    )(*args)


def _ffn_kernel(*refs, sample, tiles_per_seq):
    it = iter(refs)
    h_ref, wg_ref, wu_ref, wd_ref, cw_ref, cb_ref = (next(it) for _ in range(6))
    st_in_ref = next(it) if sample else None
    o_ref, st_out_ref = next(it), next(it)
    carry_ref = None if sample else next(it)

    i = pl.program_id(0)
    f = pl.program_id(1)
    bm = BM_FFN
    h = h_ref[...]
    g = jnp.dot(h, wg_ref[...], preferred_element_type=F32)
    u = jnp.dot(h, wu_ref[...], preferred_element_type=F32)
    if sample:
        p1, p2 = _shift_rows_sample(g, st_in_ref[...], 128)
        st_out_ref[...] = g[6 * 128:].reshape(2, 128, TF)
    else:
        first = (i % tiles_per_seq) == 0
        carry = jnp.where(first, 0.0, carry_ref[f])
        p1, p2 = _shift_rows_prompt(g, carry)
        carry_ref[f] = g[bm - 8:bm]
        st_out_ref[...] = g[bm - 2:bm]
    gc = _conv3(g, p1, p2, cw_ref[...], cb_ref[...])
    a = (gc / (1.0 + jnp.exp(-gc)) * u).astype(BF16)

    @pl.when(f == 0)
    def _():
        o_ref[...] = jnp.zeros_like(o_ref)

    for n in range(D_MODEL // TN):
        o_ref[:, n * TN:(n + 1) * TN] += jnp.dot(
            a, wd_ref[:, n * TN:(n + 1) * TN], preferred_element_type=F32)


def _ffn(h2, l, p, *, sample, st_in=None):
    rows = h2.shape[0]
    n_tiles = rows // BM_FFN
    n_f = D_FF // TF
    args = [h2, p["w_gate"], p["w_up"], p["w_down"], p["conv_ffn_w"], p["conv_ffn_b"]]
    specs = [pl.BlockSpec((BM_FFN, D_MODEL), lambda i, f: (i, 0)),
             pl.BlockSpec((None, D_MODEL, TF), lambda i, f: (l, 0, f)),
             pl.BlockSpec((None, D_MODEL, TF), lambda i, f: (l, 0, f)),
             pl.BlockSpec((None, TF, D_MODEL), lambda i, f: (l, f, 0)),
             pl.BlockSpec((None, CONV_W, TF), lambda i, f: (l, 0, f)),
             pl.BlockSpec((None, 1, TF), lambda i, f: (l, 0, f))]
    out_shape = [jax.ShapeDtypeStruct((rows, D_MODEL), F32)]
    out_specs = [pl.BlockSpec((BM_FFN, D_MODEL), lambda i, f: (i, 0))]
    scratch = []
    if sample:
        args += [st_in]
        specs += [pl.BlockSpec((None, 2, 128, TF), lambda i, f: (l, 0, 0, f))]
        out_shape += [jax.ShapeDtypeStruct((2, 128, D_FF), F32)]
        out_specs += [pl.BlockSpec((2, 128, TF), lambda i, f: (0, 0, f))]
        tiles_per_seq = 1
    else:
        tiles_per_seq = 2048 // BM_FFN
        out_shape += [jax.ShapeDtypeStruct((n_tiles, 2, D_FF), F32)]
        out_specs += [pl.BlockSpec((None, 2, TF), lambda i, f: (i, 0, f))]
        scratch = [pltpu.VMEM((n_f, 8, TF), F32)]
    outs = pl.pallas_call(
        functools.partial(_ffn_kernel, sample=sample, tiles_per_seq=tiles_per_seq),
        grid=(n_tiles, n_f),
        in_specs=specs,
        out_specs=out_specs,
        out_shape=out_shape,
        scratch_shapes=scratch,
        compiler_params=pltpu.CompilerParams(
            dimension_semantics=("arbitrary", "arbitrary"), vmem_limit_bytes=VMEM_LIMIT_BYTES),
        name="ffn_sample" if sample else "ffn_prompt",
    )(*args)
    if sample:
        return outs
    return outs[0], outs[1][tiles_per_seq - 1::tiles_per_seq]


def _final_norm_kernel(xa_ref, xb_ref, g_ref, o_ref):
    o_ref[...] = _rms(xa_ref[...] + xb_ref[...], g_ref[...])


def _final_norm(xa, xb, g):
    rows = xa.shape[0]
    spec = pl.BlockSpec((BM_NORM, D_MODEL), lambda i: (i, 0))
    return pl.pallas_call(
        _final_norm_kernel,
        grid=(rows // BM_NORM,),
        in_specs=[spec, spec, pl.BlockSpec((1, D_MODEL), lambda i: (0, 0))],
        out_specs=spec,
        out_shape=jax.ShapeDtypeStruct((rows, D_MODEL), F32),
        compiler_params=pltpu.CompilerParams(dimension_semantics=("arbitrary",)),
        name="final_norm",
    )(xa, xb, g)


def kernel(x_prompt, x_sample, state_conv_mix, state_conv_ffn, norm_mix_g, w_in, sgu_norm_g, w_s, b_s,
           conv_mix_w, conv_mix_b, out_norm_a_g, out_norm_b_g, w_out, norm_ffn_g, w_gate, w_up,
           conv_ffn_w, conv_ffn_b, w_down, final_norm_g):
    depth = w_in.shape[0]
    n_seq, seq, _ = x_prompt.shape
    n_dec, dec_seq, _ = x_sample.shape
    assert (seq, n_dec, dec_seq) == (2048, 128, 8)

    row3 = lambda a: a.reshape(depth, 1, a.shape[-1])
    p = {
        "norm_mix_g": row3(norm_mix_g), "sgu_norm_g": row3(sgu_norm_g),
        "out_norm_a_g": row3(out_norm_a_g), "out_norm_b_g": row3(out_norm_b_g),
        "norm_ffn_g": row3(norm_ffn_g), "conv_mix_b": row3(conv_mix_b), "conv_ffn_b": row3(conv_ffn_b),
        "conv_mix_w": conv_mix_w, "conv_ffn_w": conv_ffn_w,
        "w_in": w_in.astype(BF16), "w_out": w_out.astype(BF16),
        "w_gate": w_gate.astype(BF16), "w_up": w_up.astype(BF16), "w_down": w_down.astype(BF16),
        "w_s": w_s,
        "b_s_full": jnp.broadcast_to(b_s[..., None], (depth, N_HEADS, CHUNK, HEAD_DIM)),
        "w8": w_s[:, :, :dec_seq, :dec_seq].reshape(depth, -1),
        "b8": b_s[:, :, :dec_seq].reshape(depth, -1),
    }
    tmajor = lambda a: jnp.swapaxes(a, -3, -2)
    xs3 = tmajor(x_sample)
    st_mix = tmajor(state_conv_mix)
    st_ffn = tmajor(state_conv_ffn)

    xp_a, xp_b = x_prompt.reshape(n_seq * seq, D_MODEL), None
    xs_a, xs_b = xs3, None
    mix_p, ffn_p, mix_s, ffn_s, v_s = [], [], [], [], []
    for l in range(depth):
        y_p, nm_p = _in_mix(xp_a, xp_b, l, p, sample=False)
        y_s, nm_s, v_rows = _in_mix(xs_a, xs_b, l, p, sample=True, st_in=st_mix)
        xs_a2 = xs_a.reshape(dec_seq * n_dec, D_MODEL)
        xs_b2 = None if xs_b is None else xs_b.reshape(dec_seq * n_dec, D_MODEL)
        x1_p, h2_p = _out_proj(xp_a, xp_b, y_p, l, p)
        x1_s, h2_s = _out_proj(xs_a2, xs_b2, y_s.reshape(dec_seq * n_dec, D_MODEL), l, p)
        d_p, nf_p = _ffn(h2_p, l, p, sample=False)
        d_s, nf_s = _ffn(h2_s, l, p, sample=True, st_in=st_ffn)
        xp_a, xp_b = x1_p, d_p
        xs_a, xs_b = x1_s.reshape(dec_seq, n_dec, D_MODEL), d_s.reshape(dec_seq, n_dec, D_MODEL)
        mix_p.append(nm_p)
        ffn_p.append(nf_p)
        mix_s.append(nm_s)
        ffn_s.append(nf_s)
        v_s.append(v_rows)
    y_prompt = _final_norm(xp_a, xp_b, final_norm_g.reshape(1, D_MODEL)).reshape(n_seq, seq, D_MODEL)
    y_sample = _final_norm(xs_a.reshape(dec_seq * n_dec, D_MODEL), xs_b.reshape(dec_seq * n_dec, D_MODEL),
                           final_norm_g.reshape(1, D_MODEL))
    y_sample = tmajor(y_sample.reshape(dec_seq, n_dec, D_MODEL))
    return (y_prompt, y_sample, jnp.stack(mix_p), jnp.stack(ffn_p),
            tmajor(jnp.stack(mix_s)), tmajor(jnp.stack(ffn_s)), tmajor(jnp.stack(v_s)))
```

```python
import functools

import jax
import jax.numpy as jnp
from jax import lax
from jax.experimental import pallas as pl
from jax.experimental.pallas import tpu as pltpu

D_MODEL = 2048
A_WIDTH = 1024
B_WIDTH = 1024
HEAD_DIM = 128
N_HEADS = 8
CHUNK = 128
CONV_W = 3
D_FF = 5632
IN_COLS = 2 * A_WIDTH + 3 * B_WIDTH
EPS = 1e-6

VMEM_LIMIT_BYTES = 56 * 1024 * 1024

BM_IN = 256
BS_IN = BM_IN // 8
BM_FFN = 1024
TF_PROMPT = 512
TF_SAMPLE = 256
TN = 512
RES_W = 256
N_RES = D_MODEL // RES_W

F32 = jnp.float32
BF16 = jnp.bfloat16


def _rms(x, g):
    ms = jnp.mean(x * x, axis=-1, keepdims=True)
    return x * lax.rsqrt(ms + EPS) * g


def _gelu(x):
    c = 0.7978845608028654
    return 0.5 * x * (1.0 + jnp.tanh(c * (x + 0.044715 * (x * x * x))))


def _group_norm(y, g):
    outs = []
    for hd in range(N_HEADS):
        blk = y[:, hd * HEAD_DIM:(hd + 1) * HEAD_DIM]
        ms = jnp.mean(blk * blk, axis=-1, keepdims=True)
        outs.append(blk * lax.rsqrt(ms + EPS))
    return jnp.concatenate(outs, axis=-1) * g


def _shift_rows_prompt(z, carry):
    r1 = pltpu.roll(z, 1, 0)
    r2 = pltpu.roll(z, 2, 0)
    c1 = pltpu.roll(carry, 1, 0)
    c2 = pltpu.roll(carry, 2, 0)
    rid = lax.broadcasted_iota(jnp.int32, carry.shape, 0)
    top1 = jnp.where(rid < 1, c1, r1[0:8])
    top2 = jnp.where(rid < 2, c2, r2[0:8])
    p1 = jnp.concatenate([top1, r1[8:]], axis=0)
    p2 = jnp.concatenate([top2, r2[8:]], axis=0)
    return p1, p2


def _shift_rows_sample(z, st, bs):
    p1 = jnp.concatenate([st[1], z[:7 * bs]], axis=0)
    p2 = jnp.concatenate([st[0], st[1], z[:6 * bs]], axis=0)
    return p1, p2


def _conv3(z, p1, p2, cw, cb):
    return cb + cw[0:1] * p2 + cw[1:2] * p1 + cw[2:3] * z


def _row_spec(l, n):
    return pl.BlockSpec((None, 1, n), lambda i, l=l: (l, 0, 0))


def _mixer_kernel(*refs, sample, tiles_per_seq):
    it = iter(refs)
    x_ref, g1_ref, w_in_ref, gv_ref = next(it), next(it), next(it), next(it)
    if sample:
        w8_ref, b8_ref, st_in_ref = next(it), next(it), next(it)
    else:
        ws_ref, bs_ref = next(it), next(it)
    cw_ref, cb_ref, ga_ref, gb_ref, w_out_ref, g2_ref = (next(it) for _ in range(6))
    x1_ref, h2_ref, st_out_ref = next(it), next(it), next(it)
    if sample:
        v_out_ref = next(it)
    else:
        carry_ref = next(it)

    bm = BM_IN
    x = x_ref[...].reshape(bm, D_MODEL)
    h = _rms(x, g1_ref[...]).astype(BF16)

    def proj(c0):
        return jnp.dot(h, w_in_ref[:, c0:c0 + 1024], preferred_element_type=F32)

    v = _rms(_gelu(proj(A_WIDTH)), gv_ref[...])
    u = _gelu(proj(0))

    if sample:
        v_out_ref[...] = v.reshape(8, BS_IN, A_WIDTH)
        heads = []
        for hd in range(N_HEADS):
            vh = [v[t * BS_IN:(t + 1) * BS_IN, hd * HEAD_DIM:(hd + 1) * HEAD_DIM] for t in range(8)]
            rows = []
            for i in range(8):
                acc = w8_ref[(hd * 8 + i) * 8] * vh[0]
                for j in range(1, i + 1):
                    acc = acc + w8_ref[(hd * 8 + i) * 8 + j] * vh[j]
                rows.append(acc + b8_ref[hd * 8 + i])
            heads.append(jnp.concatenate(rows, axis=0))
        mixed = jnp.concatenate(heads, axis=1)
    else:
        nch = bm // CHUNK
        vb = v.astype(BF16)
        row = lax.broadcasted_iota(jnp.int32, (CHUNK, CHUNK), 0)
        col = lax.broadcasted_iota(jnp.int32, (CHUNK, CHUNK), 1)
        heads = []
        for hd in range(N_HEADS):
            wm = jnp.where(row >= col, ws_ref[hd], 0.0).astype(BF16)
            vh = jnp.concatenate(
                [vb[c * CHUNK:(c + 1) * CHUNK, hd * HEAD_DIM:(hd + 1) * HEAD_DIM] for c in range(nch)], axis=1)
            mh = jnp.dot(wm, vh, preferred_element_type=F32)
            bias = bs_ref[hd]
            heads.append(jnp.concatenate(
                [mh[:, c * HEAD_DIM:(c + 1) * HEAD_DIM] + bias for c in range(nch)], axis=0))
        mixed = jnp.concatenate(heads, axis=1)
    y_a = _group_norm(u * mixed, ga_ref[...]).astype(BF16)

    z = proj(2 * A_WIDTH + B_WIDTH) * proj(2 * A_WIDTH + 2 * B_WIDTH)
    if sample:
        p1, p2 = _shift_rows_sample(z, st_in_ref[...], BS_IN)
        st_out_ref[...] = z[6 * BS_IN:].reshape(2, BS_IN, B_WIDTH)
    else:
        first = (pl.program_id(0) % tiles_per_seq) == 0
        carry = jnp.where(first, 0.0, carry_ref[...])
        p1, p2 = _shift_rows_prompt(z, carry)
        carry_ref[...] = z[bm - 8:bm]
        st_out_ref[...] = z[bm - 2:bm]
    zc = _conv3(z, p1, p2, cw_ref[...], cb_ref[...])
    y_b = _group_norm(proj(2 * A_WIDTH) * zc, gb_ref[...]).astype(BF16)

    x1 = (x + jnp.dot(y_a, w_out_ref[0:A_WIDTH, :], preferred_element_type=F32)
          + jnp.dot(y_b, w_out_ref[A_WIDTH:, :], preferred_element_type=F32))
    h2 = _rms(x1, g2_ref[...]).astype(BF16)
    x1_ref[...] = x1.reshape(x1_ref.shape)
    h2_ref[...] = h2.reshape(h2_ref.shape)


def _mixer(x, l, p, *, sample, st_in=None):
    rows = 8 * 128 if sample else x.shape[0]
    n_tiles = rows // BM_IN
    if sample:
        x_spec = pl.BlockSpec((8, BS_IN, D_MODEL), lambda i: (0, i, 0))
    else:
        x_spec = pl.BlockSpec((BM_IN, D_MODEL), lambda i: (i, 0))
    args = [x, p["norm_mix_g"], p["w_in"], p["sgu_norm_g"]]
    specs = [x_spec, _row_spec(l, D_MODEL),
             pl.BlockSpec((None, D_MODEL, IN_COLS), lambda i: (l, 0, 0), pipeline_mode=pl.Buffered(1)),
             _row_spec(l, A_WIDTH)]
    if sample:
        args += [p["w8"][l], p["b8"][l], st_in]
        specs += [pl.BlockSpec(memory_space=pltpu.SMEM), pl.BlockSpec(memory_space=pltpu.SMEM),
                  pl.BlockSpec((None, 2, BS_IN, B_WIDTH), lambda i: (l, 0, i, 0))]
    else:
        args += [p["w_s"], p["b_s_full"]]
        specs += [pl.BlockSpec((None, N_HEADS, CHUNK, CHUNK), lambda i: (l, 0, 0, 0)),
                  pl.BlockSpec((None, N_HEADS, CHUNK, HEAD_DIM), lambda i: (l, 0, 0, 0))]
    args += [p["conv_mix_w"], p["conv_mix_b"], p["out_norm_a_g"], p["out_norm_b_g"],
             p["w_out"], p["norm_ffn_g"]]
    specs += [pl.BlockSpec((None, CONV_W, B_WIDTH), lambda i: (l, 0, 0)),
              _row_spec(l, B_WIDTH), _row_spec(l, A_WIDTH), _row_spec(l, B_WIDTH),
              pl.BlockSpec((None, D_MODEL, D_MODEL), lambda i: (l, 0, 0), pipeline_mode=pl.Buffered(1)),
              _row_spec(l, D_MODEL)]

    out_shape = [jax.ShapeDtypeStruct(x.shape, F32), jax.ShapeDtypeStruct(x.shape, BF16)]
    out_specs = [x_spec, x_spec]
    scratch = []
    if sample:
        out_shape += [jax.ShapeDtypeStruct((2, 128, B_WIDTH), F32),
                      jax.ShapeDtypeStruct((8, 128, A_WIDTH), F32)]
        out_specs += [pl.BlockSpec((2, BS_IN, B_WIDTH), lambda i: (0, i, 0)),
                      pl.BlockSpec((8, BS_IN, A_WIDTH), lambda i: (0, i, 0))]
        tiles_per_seq = 1
    else:
        tiles_per_seq = 2048 // BM_IN
        out_shape += [jax.ShapeDtypeStruct((rows // 2048, 2, B_WIDTH), F32)]
        out_specs += [pl.BlockSpec((None, 2, B_WIDTH), lambda i: (i // tiles_per_seq, 0, 0))]
        scratch = [pltpu.VMEM((8, B_WIDTH), F32)]

    return pl.pallas_call(
        functools.partial(_mixer_kernel, sample=sample, tiles_per_seq=tiles_per_seq),
        grid=(n_tiles,),
        in_specs=specs,
        out_specs=out_specs,
        out_shape=out_shape,
        scratch_shapes=scratch,
        compiler_params=pltpu.CompilerParams(
            dimension_semantics=("arbitrary",), vmem_limit_bytes=VMEM_LIMIT_BYTES),
        name="mixer_sample" if sample else "mixer_prompt",
    )(*args)


def _ffn_kernel(*refs, sample, final, tf, n_f, tiles_per_seq):
    it = iter(refs)
    h_ref, xr_ref, wg_ref, wu_ref, wd_ref, cw_ref, cb_ref = (next(it) for _ in range(7))
    st_in_ref = next(it) if sample else None
    gf_ref = next(it) if final else None
    o_ref, st_out_ref = next(it), next(it)
    if sample:
        wg_out_ref, wu_out_ref, wd_out_ref = next(it), next(it), next(it)
    else:
        carry_ref = next(it)

    i = pl.program_id(0)
    f = pl.program_id(1)
    bm = BM_FFN

    @pl.when(f == 0)
    def _():
        o_ref[...] = jnp.zeros_like(o_ref)

    @pl.when(f < N_RES)
    def _():
        c0 = pl.multiple_of(f * RES_W, RES_W)
        o_ref[:, pl.ds(c0, RES_W)] += xr_ref[...]

    if sample:
        wg = wg_ref[...].astype(BF16)
        wu = wu_ref[...].astype(BF16)
        wd = wd_ref[...].astype(BF16)
        wg_out_ref[...] = wg
        wu_out_ref[...] = wu
        wd_out_ref[...] = wd
    else:
        wg, wu, wd = wg_ref[...], wu_ref[...], wd_ref[...]
    h = h_ref[...]
    g = jnp.dot(h, wg, preferred_element_type=F32)
    u = jnp.dot(h, wu, preferred_element_type=F32)
    if sample:
        p1, p2 = _shift_rows_sample(g, st_in_ref[...], 128)
        st_out_ref[...] = g[6 * 128:].reshape(2, 128, tf)
    else:
        first = (i % tiles_per_seq) == 0
        carry = jnp.where(first, 0.0, carry_ref[f])
        p1, p2 = _shift_rows_prompt(g, carry)
        carry_ref[f] = g[bm - 8:bm]
        st_out_ref[...] = g[bm - 2:bm]
    gc = _conv3(g, p1, p2, cw_ref[...], cb_ref[...])
    a = (gc / (1.0 + jnp.exp(-gc)) * u).astype(BF16)
    for n in range(D_MODEL // TN):
        o_ref[:, n * TN:(n + 1) * TN] += jnp.dot(
            a, wd[:, n * TN:(n + 1) * TN], preferred_element_type=F32)

    if final:
        @pl.when(f == n_f - 1)
        def _():
            o_ref[...] = _rms(o_ref[...], gf_ref[...])


def _ffn(h2, x1, l, p, *, sample, final, w_bf16=None):
    rows = h2.shape[0]
    n_tiles = rows // BM_FFN
    tf = TF_SAMPLE if sample else TF_PROMPT
    n_f = D_FF // tf
    assert n_f >= N_RES
    if sample:
        w_args = [p["w_gate"], p["w_up"], p["w_down"]]
        w_specs = [pl.BlockSpec((None, D_MODEL, tf), lambda i, f: (l, 0, f)),
                   pl.BlockSpec((None, D_MODEL, tf), lambda i, f: (l, 0, f)),
                   pl.BlockSpec((None, tf, D_MODEL), lambda i, f: (l, f, 0))]
    else:
        w_args = list(w_bf16)
        w_specs = [pl.BlockSpec((D_MODEL, tf), lambda i, f: (0, f)),
                   pl.BlockSpec((D_MODEL, tf), lambda i, f: (0, f)),
                   pl.BlockSpec((tf, D_MODEL), lambda i, f: (f, 0))]
    args = [h2, x1] + w_args + [p["conv_ffn_w"], p["conv_ffn_b"]]
    specs = [pl.BlockSpec((BM_FFN, D_MODEL), lambda i, f: (i, 0)),
             pl.BlockSpec((BM_FFN, RES_W), lambda i, f: (i, jnp.minimum(f, N_RES - 1)))] + w_specs + [
             pl.BlockSpec((None, CONV_W, tf), lambda i, f: (l, 0, f)),
             pl.BlockSpec((None, 1, tf), lambda i, f: (l, 0, f))]
    out_shape = [jax.ShapeDtypeStruct((rows, D_MODEL), F32)]
    out_specs = [pl.BlockSpec((BM_FFN, D_MODEL), lambda i, f: (i, 0))]
    scratch = []
    if sample:
        args += [p["st_ffn"]]
        specs += [pl.BlockSpec((None, 2, 128, tf), lambda i, f: (l, 0, 0, f))]
    if final:
        args += [p["final_norm_g"]]
        specs += [pl.BlockSpec((1, D_MODEL), lambda i, f: (0, 0))]
    if sample:
        out_shape += [jax.ShapeDtypeStruct((2, 128, D_FF), F32),
                      jax.ShapeDtypeStruct((D_MODEL, D_FF), BF16),
                      jax.ShapeDtypeStruct((D_MODEL, D_FF), BF16),
                      jax.ShapeDtypeStruct((D_FF, D_MODEL), BF16)]
        out_specs += [pl.BlockSpec((2, 128, tf), lambda i, f: (0, 0, f)),
                      pl.BlockSpec((D_MODEL, tf), lambda i, f: (0, f)),
                      pl.BlockSpec((D_MODEL, tf), lambda i, f: (0, f)),
                      pl.BlockSpec((tf, D_MODEL), lambda i, f: (f, 0))]
        tiles_per_seq = 1
    else:
        tiles_per_seq = 2048 // BM_FFN
        out_shape += [jax.ShapeDtypeStruct((n_tiles, 2, D_FF), F32)]
        out_specs += [pl.BlockSpec((None, 2, tf), lambda i, f: (i, 0, f))]
        scratch = [pltpu.VMEM((n_f, 8, tf), F32)]
    outs = pl.pallas_call(
        functools.partial(_ffn_kernel, sample=sample, final=final, tf=tf, n_f=n_f,
                          tiles_per_seq=tiles_per_seq),
        grid=(n_tiles, n_f),
        in_specs=specs,
        out_specs=out_specs,
        out_shape=out_shape,
        scratch_shapes=scratch,
        compiler_params=pltpu.CompilerParams(
            dimension_semantics=("arbitrary", "arbitrary"), vmem_limit_bytes=VMEM_LIMIT_BYTES),
        name="ffn_sample" if sample else "ffn_prompt",
    )(*args)
    if sample:
        return outs[0], outs[1], outs[2:]
    return outs[0], outs[1][tiles_per_seq - 1::tiles_per_seq]


def kernel(x_prompt, x_sample, state_conv_mix, state_conv_ffn, norm_mix_g, w_in, sgu_norm_g, w_s, b_s,
           conv_mix_w, conv_mix_b, out_norm_a_g, out_norm_b_g, w_out, norm_ffn_g, w_gate, w_up,
           conv_ffn_w, conv_ffn_b, w_down, final_norm_g):
    depth = w_in.shape[0]
    n_seq, seq, _ = x_prompt.shape
    n_dec, dec_seq, _ = x_sample.shape
    assert (seq, n_dec, dec_seq) == (2048, 128, 8)

    row3 = lambda a: a.reshape(depth, 1, a.shape[-1])
    tmajor = lambda a: jnp.swapaxes(a, -3, -2)
    p = {
        "norm_mix_g": row3(norm_mix_g), "sgu_norm_g": row3(sgu_norm_g),
        "out_norm_a_g": row3(out_norm_a_g), "out_norm_b_g": row3(out_norm_b_g),
        "norm_ffn_g": row3(norm_ffn_g), "conv_mix_b": row3(conv_mix_b), "conv_ffn_b": row3(conv_ffn_b),
        "conv_mix_w": conv_mix_w, "conv_ffn_w": conv_ffn_w,
        "w_in": w_in.astype(BF16), "w_out": w_out.astype(BF16),
        "w_gate": w_gate, "w_up": w_up, "w_down": w_down,
        "w_s": w_s,
        "b_s_full": jnp.broadcast_to(b_s[..., None], (depth, N_HEADS, CHUNK, HEAD_DIM)),
        "w8": w_s[:, :, :dec_seq, :dec_seq].reshape(depth, -1),
        "b8": b_s[:, :, :dec_seq].reshape(depth, -1),
        "st_ffn": tmajor(state_conv_ffn),
        "final_norm_g": final_norm_g.reshape(1, D_MODEL),
    }
    st_mix = tmajor(state_conv_mix)

    xp = x_prompt.reshape(n_seq * seq, D_MODEL)
    xs = tmajor(x_sample)
    flat = lambda a: a.reshape(dec_seq * n_dec, D_MODEL)
    mix_p, ffn_p, mix_s, ffn_s, v_s = [], [], [], [], []
    for l in range(depth):
        final = l == depth - 1
        x1_s, h2_s, nm_s, v_rows = _mixer(xs, l, p, sample=True, st_in=st_mix)
        x2_s, nf_s, w_bf16 = _ffn(flat(h2_s), flat(x1_s), l, p, sample=True, final=final)
        x1_p, h2_p, nm_p = _mixer(xp, l, p, sample=False)
        xp, nf_p = _ffn(h2_p, x1_p, l, p, sample=False, final=final, w_bf16=w_bf16)
        xs = x2_s.reshape(dec_seq, n_dec, D_MODEL)
        mix_p.append(nm_p)
        ffn_p.append(nf_p)
        mix_s.append(nm_s)
        ffn_s.append(nf_s)
        v_s.append(v_rows)
    y_prompt = xp.reshape(n_seq, seq, D_MODEL)
    y_sample = tmajor(xs)
    return (y_prompt, y_sample, jnp.stack(mix_p), jnp.stack(ffn_p),
            tmajor(jnp.stack(mix_s)), tmajor(jnp.stack(ffn_s)), tmajor(jnp.stack(v_s)))
```

```python
import functools

import jax
import jax.numpy as jnp
from jax import lax
from jax.experimental import pallas as pl
from jax.experimental.pallas import tpu as pltpu

D_MODEL = 2048
A_WIDTH = 1024
B_WIDTH = 1024
HEAD_DIM = 128
N_HEADS = 8
CHUNK = 128
CONV_W = 3
D_FF = 5632
IN_COLS = 2 * A_WIDTH + 3 * B_WIDTH
EPS = 1e-6

VMEM_LIMIT_BYTES = 56 * 1024 * 1024
VMEM_LIMIT_MIXER_PROMPT_BYTES = 60 * 1024 * 1024

BM_MIX_PROMPT = 512
BM_MIX_SAMPLE = 256
BM_FFN = 1024
TF_PROMPT = 512
TF_SAMPLE = 256
TN = 512
RES_W = 256
N_RES = D_MODEL // RES_W
CAST_ROWS = 32
N_CAST = D_MODEL // CAST_ROWS

F32 = jnp.float32
BF16 = jnp.bfloat16


def _rms(x, g):
    ms = jnp.mean(x * x, axis=-1, keepdims=True)
    return x * lax.rsqrt(ms + EPS) * g


def _gelu(x):
    c = 0.7978845608028654
    return 0.5 * x * (1.0 + jnp.tanh(c * (x + 0.044715 * (x * x * x))))


def _group_norm(y, g):
    outs = []
    for hd in range(N_HEADS):
        blk = y[:, hd * HEAD_DIM:(hd + 1) * HEAD_DIM]
        ms = jnp.mean(blk * blk, axis=-1, keepdims=True)
        outs.append(blk * lax.rsqrt(ms + EPS))
    return jnp.concatenate(outs, axis=-1) * g


def _shift_rows_prompt(z, carry):
    r1 = pltpu.roll(z, 1, 0)
    r2 = pltpu.roll(z, 2, 0)
    c1 = pltpu.roll(carry, 1, 0)
    c2 = pltpu.roll(carry, 2, 0)
    rid = lax.broadcasted_iota(jnp.int32, carry.shape, 0)
    top1 = jnp.where(rid < 1, c1, r1[0:8])
    top2 = jnp.where(rid < 2, c2, r2[0:8])
    p1 = jnp.concatenate([top1, r1[8:]], axis=0)
    p2 = jnp.concatenate([top2, r2[8:]], axis=0)
    return p1, p2


def _shift_rows_sample(z, st, bs):
    p1 = jnp.concatenate([st[1], z[:7 * bs]], axis=0)
    p2 = jnp.concatenate([st[0], st[1], z[:6 * bs]], axis=0)
    return p1, p2


def _conv3(z, p1, p2, cw, cb):
    return cb + cw[0:1] * p2 + cw[1:2] * p1 + cw[2:3] * z


def _row_spec(l, n):
    return pl.BlockSpec((None, 1, n), lambda i, l=l: (l, 0, 0))


def _mixer_kernel(*refs, sample, bm, tiles_per_seq):
    it = iter(refs)
    x_ref, g1_ref, w_in_ref, gv_ref = next(it), next(it), next(it), next(it)
    if sample:
        w8_ref, b8_ref, st_in_ref = next(it), next(it), next(it)
    else:
        ws_ref, bs_ref = next(it), next(it)
    cw_ref, cb_ref, ga_ref, gb_ref, w_out_ref, g2_ref = (next(it) for _ in range(6))
    x1_ref, h2_ref, st_out_ref = next(it), next(it), next(it)
    if sample:
        v_out_ref = next(it)
    else:
        carry_ref = next(it)

    bs = bm // 8
    x = x_ref[...].reshape(bm, D_MODEL)
    h = _rms(x, g1_ref[...]).astype(BF16)

    def proj(c0):
        return jnp.dot(h, w_in_ref[:, c0:c0 + 1024], preferred_element_type=F32)

    v = _rms(_gelu(proj(A_WIDTH)), gv_ref[...])
    u = _gelu(proj(0))

    if sample:
        v_out_ref[...] = v.reshape(8, bs, A_WIDTH)
        heads = []
        for hd in range(N_HEADS):
            vh = [v[t * bs:(t + 1) * bs, hd * HEAD_DIM:(hd + 1) * HEAD_DIM] for t in range(8)]
            rows = []
            for i in range(8):
                acc = w8_ref[(hd * 8 + i) * 8] * vh[0]
                for j in range(1, i + 1):
                    acc = acc + w8_ref[(hd * 8 + i) * 8 + j] * vh[j]
                rows.append(acc + b8_ref[hd * 8 + i])
            heads.append(jnp.concatenate(rows, axis=0))
        mixed = jnp.concatenate(heads, axis=1)
    else:
        nch = bm // CHUNK
        vb = v.astype(BF16)
        row = lax.broadcasted_iota(jnp.int32, (CHUNK, CHUNK), 0)
        col = lax.broadcasted_iota(jnp.int32, (CHUNK, CHUNK), 1)
        heads = []
        for hd in range(N_HEADS):
            wm = jnp.where(row >= col, ws_ref[hd], 0.0).astype(BF16)
            vh = jnp.concatenate(
                [vb[c * CHUNK:(c + 1) * CHUNK, hd * HEAD_DIM:(hd + 1) * HEAD_DIM] for c in range(nch)], axis=1)
            mh = jnp.dot(wm, vh, preferred_element_type=F32)
            bias = bs_ref[hd]
            heads.append(jnp.concatenate(
                [mh[:, c * HEAD_DIM:(c + 1) * HEAD_DIM] + bias for c in range(nch)], axis=0))
        mixed = jnp.concatenate(heads, axis=1)
    y_a = _group_norm(u * mixed, ga_ref[...]).astype(BF16)

    z = proj(2 * A_WIDTH + B_WIDTH) * proj(2 * A_WIDTH + 2 * B_WIDTH)
    if sample:
        p1, p2 = _shift_rows_sample(z, st_in_ref[...], bs)
        st_out_ref[...] = z[6 * bs:].reshape(2, bs, B_WIDTH)
    else:
        first = (pl.program_id(0) % tiles_per_seq) == 0
        carry = jnp.where(first, 0.0, carry_ref[...])
        p1, p2 = _shift_rows_prompt(z, carry)
        carry_ref[...] = z[bm - 8:bm]
        st_out_ref[...] = z[bm - 2:bm]
    zc = _conv3(z, p1, p2, cw_ref[...], cb_ref[...])
    y_b = _group_norm(proj(2 * A_WIDTH) * zc, gb_ref[...]).astype(BF16)

    x1 = (x + jnp.dot(y_a, w_out_ref[0:A_WIDTH, :], preferred_element_type=F32)
          + jnp.dot(y_b, w_out_ref[A_WIDTH:, :], preferred_element_type=F32))
    h2 = _rms(x1, g2_ref[...]).astype(BF16)
    x1_ref[...] = x1.reshape(x1_ref.shape)
    h2_ref[...] = h2.reshape(h2_ref.shape)


def _mixer(x, w_in_b, w_out_b, l, p, *, sample, st_in=None):
    rows = 8 * 128 if sample else x.shape[0]
    bm = BM_MIX_SAMPLE if sample else BM_MIX_PROMPT
    bs = bm // 8
    n_tiles = rows // bm
    if sample:
        x_spec = pl.BlockSpec((8, bs, D_MODEL), lambda i: (0, i, 0))
    else:
        x_spec = pl.BlockSpec((bm, D_MODEL), lambda i: (i, 0))
    args = [x, p["norm_mix_g"], w_in_b, p["sgu_norm_g"]]
    specs = [x_spec, _row_spec(l, D_MODEL),
             pl.BlockSpec((D_MODEL, IN_COLS), lambda i: (0, 0), pipeline_mode=pl.Buffered(1)),
             _row_spec(l, A_WIDTH)]
    if sample:
        args += [p["w8"][l], p["b8"][l], st_in]
        specs += [pl.BlockSpec(memory_space=pltpu.SMEM), pl.BlockSpec(memory_space=pltpu.SMEM),
                  pl.BlockSpec((None, 2, bs, B_WIDTH), lambda i: (l, 0, i, 0))]
    else:
        args += [p["w_s"], p["b_s_full"]]
        specs += [pl.BlockSpec((None, N_HEADS, CHUNK, CHUNK), lambda i: (l, 0, 0, 0)),
                  pl.BlockSpec((None, N_HEADS, CHUNK, HEAD_DIM), lambda i: (l, 0, 0, 0))]
    args += [p["conv_mix_w"], p["conv_mix_b"], p["out_norm_a_g"], p["out_norm_b_g"],
             w_out_b, p["norm_ffn_g"]]
    specs += [pl.BlockSpec((None, CONV_W, B_WIDTH), lambda i: (l, 0, 0)),
              _row_spec(l, B_WIDTH), _row_spec(l, A_WIDTH), _row_spec(l, B_WIDTH),
              pl.BlockSpec((D_MODEL, D_MODEL), lambda i: (0, 0), pipeline_mode=pl.Buffered(1)),
              _row_spec(l, D_MODEL)]

    out_shape = [jax.ShapeDtypeStruct(x.shape, F32), jax.ShapeDtypeStruct(x.shape, BF16)]
    out_specs = [x_spec, x_spec]
    scratch = []
    if sample:
        out_shape += [jax.ShapeDtypeStruct((2, 128, B_WIDTH), F32),
                      jax.ShapeDtypeStruct((8, 128, A_WIDTH), F32)]
        out_specs += [pl.BlockSpec((2, bs, B_WIDTH), lambda i: (0, i, 0)),
                      pl.BlockSpec((8, bs, A_WIDTH), lambda i: (0, i, 0))]
        tiles_per_seq = 1
    else:
        tiles_per_seq = 2048 // bm
        out_shape += [jax.ShapeDtypeStruct((rows // 2048, 2, B_WIDTH), F32)]
        out_specs += [pl.BlockSpec((None, 2, B_WIDTH), lambda i: (i // tiles_per_seq, 0, 0))]
        scratch = [pltpu.VMEM((8, B_WIDTH), F32)]

    return pl.pallas_call(
        functools.partial(_mixer_kernel, sample=sample, bm=bm, tiles_per_seq=tiles_per_seq),
        grid=(n_tiles,),
        in_specs=specs,
        out_specs=out_specs,
        out_shape=out_shape,
        scratch_shapes=scratch,
        compiler_params=pltpu.CompilerParams(
            dimension_semantics=("arbitrary",),
            vmem_limit_bytes=VMEM_LIMIT_BYTES if sample else VMEM_LIMIT_MIXER_PROMPT_BYTES),
        name="mixer_sample" if sample else "mixer_prompt",
    )(*args)


def _ffn_kernel(*refs, sample, final, cast_next, tf, n_f, tiles_per_seq):
    it = iter(refs)
    h_ref, xr_ref, wg_ref, wu_ref, wd_ref, cw_ref, cb_ref = (next(it) for _ in range(7))
    st_in_ref = next(it) if sample else None
    gf_ref = next(it) if final else None
    if cast_next:
        win_ref, wout_ref = next(it), next(it)
    o_ref, st_out_ref = next(it), next(it)
    if sample:
        wg_out_ref, wu_out_ref, wd_out_ref = next(it), next(it), next(it)
    if cast_next:
        win_out_ref, wout_out_ref = next(it), next(it)
    a_ref = next(it)
    carry_ref = None if sample else next(it)

    i = pl.program_id(0)
    f = pl.program_id(1)
    bm = BM_FFN
    slot = f % 2

    def cast_rows():
        if cast_next:
            win_out_ref[...] = win_ref[...].astype(BF16)
            wout_out_ref[...] = wout_ref[...].astype(BF16)

    def gate_up():
        if sample:
            wg = wg_ref[...].astype(BF16)
            wu = wu_ref[...].astype(BF16)
            wg_out_ref[...] = wg
            wu_out_ref[...] = wu
        else:
            wg, wu = wg_ref[...], wu_ref[...]
        h = h_ref[...]
        g = jnp.dot(h, wg, preferred_element_type=F32)
        u = jnp.dot(h, wu, preferred_element_type=F32)
        if sample:
            p1, p2 = _shift_rows_sample(g, st_in_ref[...], 128)
            st_out_ref[...] = g[6 * 128:].reshape(2, 128, tf)
        else:
            first = (i % tiles_per_seq) == 0
            carry = jnp.where(first, 0.0, carry_ref[f])
            p1, p2 = _shift_rows_prompt(g, carry)
            carry_ref[f] = g[bm - 8:bm]
            st_out_ref[...] = g[bm - 2:bm]
        gc = _conv3(g, p1, p2, cw_ref[...], cb_ref[...])
        a_ref[slot] = (gc / (1.0 + jnp.exp(-gc)) * u).astype(BF16)

    def down():
        if sample:
            wd = wd_ref[...].astype(BF16)
            wd_out_ref[...] = wd
        else:
            wd = wd_ref[...]
        a = a_ref[1 - slot]
        for n in range(D_MODEL // TN):
            o_ref[:, n * TN:(n + 1) * TN] += jnp.dot(
                a, wd[:, n * TN:(n + 1) * TN], preferred_element_type=F32)

    @pl.when(f == 0)
    def _():
        o_ref[:, 0:RES_W] = xr_ref[...]
        o_ref[:, RES_W:] = jnp.zeros((bm, D_MODEL - RES_W), F32)
        cast_rows()
        gate_up()

    @pl.when((f > 0) & (f < n_f))
    def _():
        c0 = pl.multiple_of(jnp.minimum(f, N_RES - 1) * RES_W, RES_W)
        o_ref[:, pl.ds(c0, RES_W)] += jnp.where(f < N_RES, xr_ref[...], 0.0)
        cast_rows()
        gate_up()
        down()

    @pl.when(f == n_f)
    def _():
        cast_rows()
        down()
        if final:
            o_ref[...] = _rms(o_ref[...], gf_ref[...])


def _ffn(h2, x1, l, p, *, sample, final, w_bf16=None, cast_next=None):
    rows = h2.shape[0]
    n_tiles = rows // BM_FFN
    tf = TF_SAMPLE if sample else TF_PROMPT
    n_f = D_FF // tf
    assert n_f >= N_RES
    up = lambda f: jnp.minimum(f, n_f - 1)
    dn = lambda f: jnp.maximum(f - 1, 0)
    if sample:
        w_args = [p["w_gate"], p["w_up"], p["w_down"]]
        w_specs = [pl.BlockSpec((None, D_MODEL, tf), lambda i, f: (l, 0, up(f))),
                   pl.BlockSpec((None, D_MODEL, tf), lambda i, f: (l, 0, up(f))),
                   pl.BlockSpec((None, tf, D_MODEL), lambda i, f: (l, dn(f), 0))]
    else:
        w_args = list(w_bf16)
        w_specs = [pl.BlockSpec((D_MODEL, tf), lambda i, f: (0, up(f))),
                   pl.BlockSpec((D_MODEL, tf), lambda i, f: (0, up(f))),
                   pl.BlockSpec((tf, D_MODEL), lambda i, f: (dn(f), 0))]
    args = [h2, x1] + w_args + [p["conv_ffn_w"], p["conv_ffn_b"]]
    specs = [pl.BlockSpec((BM_FFN, D_MODEL), lambda i, f: (i, 0)),
             pl.BlockSpec((BM_FFN, RES_W), lambda i, f: (i, jnp.minimum(f, N_RES - 1)))] + w_specs + [
             pl.BlockSpec((None, CONV_W, tf), lambda i, f: (l, 0, up(f))),
             pl.BlockSpec((None, 1, tf), lambda i, f: (l, 0, up(f)))]
    out_shape = [jax.ShapeDtypeStruct((rows, D_MODEL), F32)]
    out_specs = [pl.BlockSpec((BM_FFN, D_MODEL), lambda i, f: (i, 0))]
    scratch = [pltpu.VMEM((2, BM_FFN, tf), BF16)]
    if sample:
        args += [p["st_ffn"]]
        specs += [pl.BlockSpec((None, 2, 128, tf), lambda i, f: (l, 0, 0, up(f)))]
    if final:
        args += [p["final_norm_g"]]
        specs += [pl.BlockSpec((1, D_MODEL), lambda i, f: (0, 0))]
    if cast_next is not None:
        assert n_tiles * (n_f + 1) >= N_CAST
        chunk = lambda i, f: (jnp.minimum(i * (n_f + 1) + f, N_CAST - 1), 0)
        chunk_next = lambda i, f: (l + 1,) + chunk(i, f)
        args += list(cast_next)
        specs += [pl.BlockSpec((None, CAST_ROWS, IN_COLS), chunk_next),
                  pl.BlockSpec((None, CAST_ROWS, D_MODEL), chunk_next)]
    if sample:
        out_shape += [jax.ShapeDtypeStruct((2, 128, D_FF), F32),
                      jax.ShapeDtypeStruct((D_MODEL, D_FF), BF16),
                      jax.ShapeDtypeStruct((D_MODEL, D_FF), BF16),
                      jax.ShapeDtypeStruct((D_FF, D_MODEL), BF16)]
        out_specs += [pl.BlockSpec((2, 128, tf), lambda i, f: (0, 0, up(f))),
                      pl.BlockSpec((D_MODEL, tf), lambda i, f: (0, up(f))),
                      pl.BlockSpec((D_MODEL, tf), lambda i, f: (0, up(f))),
                      pl.BlockSpec((tf, D_MODEL), lambda i, f: (dn(f), 0))]
        tiles_per_seq = 1
    else:
        tiles_per_seq = 2048 // BM_FFN
        out_shape += [jax.ShapeDtypeStruct((n_tiles, 2, D_FF), F32)]
        out_specs += [pl.BlockSpec((None, 2, tf), lambda i, f: (i, 0, up(f)))]
        scratch += [pltpu.VMEM((n_f, 8, tf), F32)]
    if cast_next is not None:
        out_shape += [jax.ShapeDtypeStruct((D_MODEL, IN_COLS), BF16),
                      jax.ShapeDtypeStruct((D_MODEL, D_MODEL), BF16)]
        out_specs += [pl.BlockSpec((CAST_ROWS, IN_COLS), chunk), pl.BlockSpec((CAST_ROWS, D_MODEL), chunk)]
    outs = pl.pallas_call(
        functools.partial(_ffn_kernel, sample=sample, final=final, cast_next=cast_next is not None,
                          tf=tf, n_f=n_f, tiles_per_seq=tiles_per_seq),
        grid=(n_tiles, n_f + 1),
        in_specs=specs,
        out_specs=out_specs,
        out_shape=out_shape,
        scratch_shapes=scratch,
        compiler_params=pltpu.CompilerParams(
            dimension_semantics=("arbitrary", "arbitrary"), vmem_limit_bytes=VMEM_LIMIT_BYTES),
        name="ffn_sample" if sample else "ffn_prompt",
    )(*args)
    if sample:
        return outs[0], outs[1], outs[2:]
    return outs[0], outs[1][tiles_per_seq - 1::tiles_per_seq], outs[2:]


def kernel(x_prompt, x_sample, state_conv_mix, state_conv_ffn, norm_mix_g, w_in, sgu_norm_g, w_s, b_s,
           conv_mix_w, conv_mix_b, out_norm_a_g, out_norm_b_g, w_out, norm_ffn_g, w_gate, w_up,
           conv_ffn_w, conv_ffn_b, w_down, final_norm_g):
    depth = w_in.shape[0]
    n_seq, seq, _ = x_prompt.shape
    n_dec, dec_seq, _ = x_sample.shape
    assert (seq, n_dec, dec_seq) == (2048, 128, 8)

    row3 = lambda a: a.reshape(depth, 1, a.shape[-1])
    tmajor = lambda a: jnp.swapaxes(a, -3, -2)
    p = {
        "norm_mix_g": row3(norm_mix_g), "sgu_norm_g": row3(sgu_norm_g),
        "out_norm_a_g": row3(out_norm_a_g), "out_norm_b_g": row3(out_norm_b_g),
        "norm_ffn_g": row3(norm_ffn_g), "conv_mix_b": row3(conv_mix_b), "conv_ffn_b": row3(conv_ffn_b),
        "conv_mix_w": conv_mix_w, "conv_ffn_w": conv_ffn_w,
        "w_gate": w_gate, "w_up": w_up, "w_down": w_down,
        "w_s": w_s,
        "b_s_full": jnp.broadcast_to(b_s[..., None], (depth, N_HEADS, CHUNK, HEAD_DIM)),
        "w8": w_s[:, :, :dec_seq, :dec_seq].reshape(depth, -1),
        "b8": b_s[:, :, :dec_seq].reshape(depth, -1),
        "st_ffn": tmajor(state_conv_ffn),
        "final_norm_g": final_norm_g.reshape(1, D_MODEL),
    }
    st_mix = tmajor(state_conv_mix)

    xp = x_prompt.reshape(n_seq * seq, D_MODEL)
    xs = tmajor(x_sample)
    flat = lambda a: a.reshape(dec_seq * n_dec, D_MODEL)
    proj_b = (w_in[0].astype(BF16), w_out[0].astype(BF16))
    mix_p, ffn_p, mix_s, ffn_s, v_s = [], [], [], [], []
    for l in range(depth):
        final = l == depth - 1
        x1_s, h2_s, nm_s, v_rows = _mixer(xs, *proj_b, l, p, sample=True, st_in=st_mix)
        x2_s, nf_s, w_bf16 = _ffn(flat(h2_s), flat(x1_s), l, p, sample=True, final=final)
        x1_p, h2_p, nm_p = _mixer(xp, *proj_b, l, p, sample=False)
        cast_next = None if final else (w_in, w_out)
        xp, nf_p, proj_b = _ffn(h2_p, x1_p, l, p, sample=False, final=final, w_bf16=w_bf16,
                                cast_next=cast_next)
        xs = x2_s.reshape(dec_seq, n_dec, D_MODEL)
        mix_p.append(nm_p)
        ffn_p.append(nf_p)
        mix_s.append(nm_s)
        ffn_s.append(nf_s)
        v_s.append(v_rows)
    y_prompt = xp.reshape(n_seq, seq, D_MODEL)
    y_sample = tmajor(xs)
    return (y_prompt, y_sample, jnp.stack(mix_p), jnp.stack(ffn_p),
            tmajor(jnp.stack(mix_s)), tmajor(jnp.stack(ffn_s)), tmajor(jnp.stack(v_s)))
```

```python
import functools

import jax
import jax.numpy as jnp
from jax import lax
from jax.experimental import pallas as pl
from jax.experimental.pallas import tpu as pltpu

D_MODEL = 2048
A_WIDTH = 1024
B_WIDTH = 1024
HEAD_DIM = 128
N_HEADS = 8
CHUNK = 128
CONV_W = 3
D_FF = 5632
IN_COLS = 2 * A_WIDTH + 3 * B_WIDTH
EPS = 1e-6

VMEM_LIMIT_BYTES = 56 * 1024 * 1024
VMEM_LIMIT_MIXER_PROMPT_BYTES = 60 * 1024 * 1024

BM_MIX_PROMPT = 512
BM_MIX_SAMPLE = 256
BM_FFN = 1024
TF_PROMPT = 512
TF_SAMPLE = 256
TN = 512
RES_W = 512
N_RES = D_MODEL // RES_W
CAST_ROWS = 64
N_CAST = D_MODEL // CAST_ROWS

F32 = jnp.float32
BF16 = jnp.bfloat16


def _rms(x, g):
    ms = jnp.mean(x * x, axis=-1, keepdims=True)
    return x * lax.rsqrt(ms + EPS) * g


def _gelu(x):
    c = 0.7978845608028654
    return 0.5 * x * (1.0 + jnp.tanh(c * (x + 0.044715 * (x * x * x))))


def _group_norm(y, g):
    outs = []
    for hd in range(N_HEADS):
        blk = y[:, hd * HEAD_DIM:(hd + 1) * HEAD_DIM]
        ms = jnp.mean(blk * blk, axis=-1, keepdims=True)
        outs.append(blk * lax.rsqrt(ms + EPS))
    return jnp.concatenate(outs, axis=-1) * g


def _shift_rows_prompt(z, carry):
    r1 = pltpu.roll(z, 1, 0)
    r2 = pltpu.roll(z, 2, 0)
    c1 = pltpu.roll(carry, 1, 0)
    c2 = pltpu.roll(carry, 2, 0)
    rid = lax.broadcasted_iota(jnp.int32, carry.shape, 0)
    top1 = jnp.where(rid < 1, c1, r1[0:8])
    top2 = jnp.where(rid < 2, c2, r2[0:8])
    p1 = jnp.concatenate([top1, r1[8:]], axis=0)
    p2 = jnp.concatenate([top2, r2[8:]], axis=0)
    return p1, p2


def _shift_rows_sample(z, st, bs):
    p1 = jnp.concatenate([st[1], z[:7 * bs]], axis=0)
    p2 = jnp.concatenate([st[0], st[1], z[:6 * bs]], axis=0)
    return p1, p2


def _conv3(z, p1, p2, cwb):
    return cwb[3:4] + cwb[0:1] * p2 + cwb[1:2] * p1 + cwb[2:3] * z


def _row_spec(l, n):
    return pl.BlockSpec((None, 1, n), lambda i, l=l: (l, 0, 0))


def _mixer_kernel(*refs, sample, bm, tiles_per_seq):
    it = iter(refs)
    x_ref, g1_ref, w_in_ref, gv_ref = next(it), next(it), next(it), next(it)
    if sample:
        w8_ref, b8_ref, st_in_ref = next(it), next(it), next(it)
    else:
        ws_ref, bs_ref = next(it), next(it)
    cwb_ref, ga_ref, gb_ref, w_out_ref, g2_ref = (next(it) for _ in range(5))
    x1_ref, h2_ref, st_out_ref = next(it), next(it), next(it)
    if sample:
        v_out_ref = next(it)
    else:
        carry_ref = next(it)

    bs = bm // 8
    x = x_ref[...].reshape(bm, D_MODEL)
    h = _rms(x, g1_ref[...]).astype(BF16)

    def proj(c0):
        return jnp.dot(h, w_in_ref[:, c0:c0 + 1024], preferred_element_type=F32)

    v = _rms(_gelu(proj(A_WIDTH)), gv_ref[...])
    u = _gelu(proj(0))

    if sample:
        v_out_ref[...] = v.reshape(8, bs, A_WIDTH)
        heads = []
        for hd in range(N_HEADS):
            vh = [v[t * bs:(t + 1) * bs, hd * HEAD_DIM:(hd + 1) * HEAD_DIM] for t in range(8)]
            rows = []
            for i in range(8):
                acc = w8_ref[(hd * 8 + i) * 8] * vh[0]
                for j in range(1, i + 1):
                    acc = acc + w8_ref[(hd * 8 + i) * 8 + j] * vh[j]
                rows.append(acc + b8_ref[hd * 8 + i])
            heads.append(jnp.concatenate(rows, axis=0))
        mixed = jnp.concatenate(heads, axis=1)
    else:
        nch = bm // CHUNK
        vb = v.astype(BF16)
        row = lax.broadcasted_iota(jnp.int32, (CHUNK, CHUNK), 0)
        col = lax.broadcasted_iota(jnp.int32, (CHUNK, CHUNK), 1)
        heads = []
        for hd in range(N_HEADS):
            wm = jnp.where(row >= col, ws_ref[hd], 0.0).astype(BF16)
            vh = jnp.concatenate(
                [vb[c * CHUNK:(c + 1) * CHUNK, hd * HEAD_DIM:(hd + 1) * HEAD_DIM] for c in range(nch)], axis=1)
            mh = jnp.dot(wm, vh, preferred_element_type=F32)
            bias = bs_ref[hd]
            heads.append(jnp.concatenate(
                [mh[:, c * HEAD_DIM:(c + 1) * HEAD_DIM] + bias for c in range(nch)], axis=0))
        mixed = jnp.concatenate(heads, axis=1)
    y_a = _group_norm(u * mixed, ga_ref[...]).astype(BF16)

    z = proj(2 * A_WIDTH + B_WIDTH) * proj(2 * A_WIDTH + 2 * B_WIDTH)
    if sample:
        p1, p2 = _shift_rows_sample(z, st_in_ref[...], bs)
        st_out_ref[...] = z[6 * bs:].reshape(2, bs, B_WIDTH)
    else:
        first = (pl.program_id(0) % tiles_per_seq) == 0
        carry = jnp.where(first, 0.0, carry_ref[...])
        p1, p2 = _shift_rows_prompt(z, carry)
        carry_ref[...] = z[bm - 8:bm]
        st_out_ref[...] = z[bm - 2:bm]
    zc = _conv3(z, p1, p2, cwb_ref[...])
    y_b = _group_norm(proj(2 * A_WIDTH) * zc, gb_ref[...]).astype(BF16)

    x1 = (x + jnp.dot(y_a, w_out_ref[0:A_WIDTH, :], preferred_element_type=F32)
          + jnp.dot(y_b, w_out_ref[A_WIDTH:, :], preferred_element_type=F32))
    h2 = _rms(x1, g2_ref[...]).astype(BF16)
    x1_ref[...] = x1.reshape(x1_ref.shape)
    h2_ref[...] = h2.reshape(h2_ref.shape)


def _mixer(x, w_in_b, w_out_b, l, p, *, sample, st_in=None):
    rows = 8 * 128 if sample else x.shape[0]
    bm = BM_MIX_SAMPLE if sample else BM_MIX_PROMPT
    bs = bm // 8
    n_tiles = rows // bm
    if sample:
        x_spec = pl.BlockSpec((8, bs, D_MODEL), lambda i: (0, i, 0))
    else:
        x_spec = pl.BlockSpec((bm, D_MODEL), lambda i: (i, 0))
    args = [x, p["norm_mix_g"], w_in_b, p["sgu_norm_g"]]
    specs = [x_spec, _row_spec(l, D_MODEL),
             pl.BlockSpec((D_MODEL, IN_COLS), lambda i: (0, 0), pipeline_mode=pl.Buffered(1)),
             _row_spec(l, A_WIDTH)]
    if sample:
        args += [p["w8"][l], p["b8"][l], st_in]
        specs += [pl.BlockSpec(memory_space=pltpu.SMEM), pl.BlockSpec(memory_space=pltpu.SMEM),
                  pl.BlockSpec((None, 2, bs, B_WIDTH), lambda i: (l, 0, i, 0))]
    else:
        args += [p["w_s"], p["b_s_full"]]
        specs += [pl.BlockSpec((None, N_HEADS, CHUNK, CHUNK), lambda i: (l, 0, 0, 0)),
                  pl.BlockSpec((None, N_HEADS, CHUNK, HEAD_DIM), lambda i: (l, 0, 0, 0))]
    args += [p["conv_mix_wb"], p["out_norm_a_g"], p["out_norm_b_g"], w_out_b, p["norm_ffn_g"]]
    specs += [pl.BlockSpec((None, CONV_W + 1, B_WIDTH), lambda i: (l, 0, 0)),
              _row_spec(l, A_WIDTH), _row_spec(l, B_WIDTH),
              pl.BlockSpec((D_MODEL, D_MODEL), lambda i: (0, 0), pipeline_mode=pl.Buffered(1)),
              _row_spec(l, D_MODEL)]

    out_shape = [jax.ShapeDtypeStruct(x.shape, F32), jax.ShapeDtypeStruct(x.shape, BF16)]
    out_specs = [x_spec, x_spec]
    scratch = []
    if sample:
        out_shape += [jax.ShapeDtypeStruct((2, 128, B_WIDTH), F32),
                      jax.ShapeDtypeStruct((8, 128, A_WIDTH), F32)]
        out_specs += [pl.BlockSpec((2, bs, B_WIDTH), lambda i: (0, i, 0)),
                      pl.BlockSpec((8, bs, A_WIDTH), lambda i: (0, i, 0))]
        tiles_per_seq = 1
    else:
        tiles_per_seq = 2048 // bm
        out_shape += [jax.ShapeDtypeStruct((rows // 2048, 2, B_WIDTH), F32)]
        out_specs += [pl.BlockSpec((None, 2, B_WIDTH), lambda i: (i // tiles_per_seq, 0, 0))]
        scratch = [pltpu.VMEM((8, B_WIDTH), F32)]

    return pl.pallas_call(
        functools.partial(_mixer_kernel, sample=sample, bm=bm, tiles_per_seq=tiles_per_seq),
        grid=(n_tiles,),
        in_specs=specs,
        out_specs=out_specs,
        out_shape=out_shape,
        scratch_shapes=scratch,
        compiler_params=pltpu.CompilerParams(
            dimension_semantics=("arbitrary",),
            vmem_limit_bytes=VMEM_LIMIT_BYTES if sample else VMEM_LIMIT_MIXER_PROMPT_BYTES),
        name="mixer_sample" if sample else "mixer_prompt",
    )(*args)


def _ffn_kernel(*refs, sample, final, cast_next, tf, n_f, tiles_per_seq):
    it = iter(refs)
    h_ref, xr_ref = next(it), next(it)
    if sample:
        wg_ref, wu_ref = next(it), next(it)
    else:
        wgu_ref = next(it)
    wd_ref, cwb_ref = next(it), next(it)
    st_in_ref = next(it) if sample else None
    gf_ref = next(it) if final else None
    if cast_next:
        win_ref, wout_ref = next(it), next(it)
    o_ref, st_out_ref = next(it), next(it)
    if sample:
        wgu_out_ref, wd_out_ref = next(it), next(it)
    if cast_next:
        win_out_ref, wout_out_ref = next(it), next(it)
    carry_ref = None if sample else next(it)

    i = pl.program_id(0)
    f = pl.program_id(1)
    bm = BM_FFN
    c0 = pl.multiple_of(f * tf, tf)

    @pl.when(f == 0)
    def _():
        o_ref[...] = jnp.zeros_like(o_ref)

    @pl.when(f < N_RES)
    def _():
        r0 = pl.multiple_of(f * RES_W, RES_W)
        o_ref[:, pl.ds(r0, RES_W)] += xr_ref[...]

    if cast_next:
        win_out_ref[...] = win_ref[...].astype(BF16)
        wout_out_ref[...] = wout_ref[...].astype(BF16)

    h = h_ref[...]
    if sample:
        wg = wg_ref[...].astype(BF16)
        wu = wu_ref[...].astype(BF16)
        wd = wd_ref[...].astype(BF16)
        wgu_out_ref[:, 0:tf] = wg
        wgu_out_ref[:, tf:] = wu
        wd_out_ref[...] = wd
        g = jnp.dot(h, wg, preferred_element_type=F32)
        u = jnp.dot(h, wu, preferred_element_type=F32)
        p1, p2 = _shift_rows_sample(g, st_in_ref[...], 128)
        st_out_ref[...] = g[6 * 128:].reshape(2, 128, tf)
    else:
        wd = wd_ref[...]
        gu = jnp.dot(h, wgu_ref[...], preferred_element_type=F32)
        ts = TF_SAMPLE
        g = jnp.concatenate([gu[:, 2 * k * ts:(2 * k + 1) * ts] for k in range(tf // ts)], axis=1)
        u = jnp.concatenate([gu[:, (2 * k + 1) * ts:(2 * k + 2) * ts] for k in range(tf // ts)], axis=1)
        first = (i % tiles_per_seq) == 0
        carry = jnp.where(first, 0.0, carry_ref[f])
        p1, p2 = _shift_rows_prompt(g, carry)
        carry_ref[f] = g[bm - 8:bm]
        st_out_ref[:, pl.ds(c0, tf)] = g[bm - 2:bm]
    gc = _conv3(g, p1, p2, cwb_ref[:, pl.ds(c0, tf)])
    a = (gc / (1.0 + jnp.exp(-gc)) * u).astype(BF16)
    for n in range(D_MODEL // TN):
        o_ref[:, n * TN:(n + 1) * TN] += jnp.dot(
            a, wd[:, n * TN:(n + 1) * TN], preferred_element_type=F32)

    if final:
        @pl.when(f == n_f - 1)
        def _():
            o_ref[...] = _rms(o_ref[...], gf_ref[...])


def _ffn(h2, x1, l, p, *, sample, final, w_bf16=None, cast_next=None):
    rows = h2.shape[0]
    n_tiles = rows // BM_FFN
    tf = TF_SAMPLE if sample else TF_PROMPT
    n_f = D_FF // tf
    assert n_f >= N_RES
    if sample:
        w_args = [p["w_gate"], p["w_up"], p["w_down"]]
        w_specs = [pl.BlockSpec((None, D_MODEL, tf), lambda i, f: (l, 0, f)),
                   pl.BlockSpec((None, D_MODEL, tf), lambda i, f: (l, 0, f)),
                   pl.BlockSpec((None, tf, D_MODEL), lambda i, f: (l, f, 0))]
    else:
        w_args = list(w_bf16)
        w_specs = [pl.BlockSpec((D_MODEL, 2 * tf), lambda i, f: (0, f)),
                   pl.BlockSpec((tf, D_MODEL), lambda i, f: (f, 0))]
    args = [h2, x1] + w_args + [p["conv_ffn_wb"]]
    specs = [pl.BlockSpec((BM_FFN, D_MODEL), lambda i, f: (i, 0)),
             pl.BlockSpec((BM_FFN, RES_W), lambda i, f: (i, jnp.minimum(f, N_RES - 1)))] + w_specs + [
             pl.BlockSpec((None, CONV_W + 1, D_FF), lambda i, f: (l, 0, 0))]
    out_shape = [jax.ShapeDtypeStruct((rows, D_MODEL), F32)]
    out_specs = [pl.BlockSpec((BM_FFN, D_MODEL), lambda i, f: (i, 0))]
    scratch = []
    if sample:
        args += [p["st_ffn"]]
        specs += [pl.BlockSpec((None, 2, 128, tf), lambda i, f: (l, 0, 0, f))]
    if final:
        args += [p["final_norm_g"]]
        specs += [pl.BlockSpec((1, D_MODEL), lambda i, f: (0, 0))]
    if cast_next is not None:
        chunks_per_tile = N_CAST // n_tiles
        steps_per_chunk = -(-n_f // chunks_per_tile)
        chunk = lambda i, f: (l + 1, i * chunks_per_tile + f // steps_per_chunk, 0)
        args += list(cast_next)
        specs += [pl.BlockSpec((None, CAST_ROWS, IN_COLS), chunk),
                  pl.BlockSpec((None, CAST_ROWS, D_MODEL), chunk)]
    if sample:
        out_shape += [jax.ShapeDtypeStruct((2, 128, D_FF), F32),
                      jax.ShapeDtypeStruct((D_MODEL, 2 * D_FF), BF16),
                      jax.ShapeDtypeStruct((D_FF, D_MODEL), BF16)]
        out_specs += [pl.BlockSpec((2, 128, tf), lambda i, f: (0, 0, f)),
                      pl.BlockSpec((D_MODEL, 2 * tf), lambda i, f: (0, f)),
                      pl.BlockSpec((tf, D_MODEL), lambda i, f: (f, 0))]
        tiles_per_seq = 1
    else:
        tiles_per_seq = 2048 // BM_FFN
        out_shape += [jax.ShapeDtypeStruct((n_tiles, 2, D_FF), F32)]
        out_specs += [pl.BlockSpec((None, 2, D_FF), lambda i, f: (i, 0, 0))]
        scratch = [pltpu.VMEM((n_f, 8, tf), F32)]
    if cast_next is not None:
        out_shape += [jax.ShapeDtypeStruct((D_MODEL, IN_COLS), BF16),
                      jax.ShapeDtypeStruct((D_MODEL, D_MODEL), BF16)]
        out_specs += [pl.BlockSpec((CAST_ROWS, IN_COLS), lambda i, f: chunk(i, f)[1:]),
                      pl.BlockSpec((CAST_ROWS, D_MODEL), lambda i, f: chunk(i, f)[1:])]
    outs = pl.pallas_call(
        functools.partial(_ffn_kernel, sample=sample, final=final, cast_next=cast_next is not None,
                          tf=tf, n_f=n_f, tiles_per_seq=tiles_per_seq),
        grid=(n_tiles, n_f),
        in_specs=specs,
        out_specs=out_specs,
        out_shape=out_shape,
        scratch_shapes=scratch,
        compiler_params=pltpu.CompilerParams(
            dimension_semantics=("arbitrary", "arbitrary"), vmem_limit_bytes=VMEM_LIMIT_BYTES),
        name="ffn_sample" if sample else "ffn_prompt",
    )(*args)
    if sample:
        return outs[0], outs[1], outs[2:]
    return outs[0], outs[1][tiles_per_seq - 1::tiles_per_seq], outs[2:]


def kernel(x_prompt, x_sample, state_conv_mix, state_conv_ffn, norm_mix_g, w_in, sgu_norm_g, w_s, b_s,
           conv_mix_w, conv_mix_b, out_norm_a_g, out_norm_b_g, w_out, norm_ffn_g, w_gate, w_up,
           conv_ffn_w, conv_ffn_b, w_down, final_norm_g):
    depth = w_in.shape[0]
    n_seq, seq, _ = x_prompt.shape
    n_dec, dec_seq, _ = x_sample.shape
    assert (seq, n_dec, dec_seq) == (2048, 128, 8)

    row3 = lambda a: a.reshape(depth, 1, a.shape[-1])
    tmajor = lambda a: jnp.swapaxes(a, -3, -2)
    p = {
        "norm_mix_g": row3(norm_mix_g), "sgu_norm_g": row3(sgu_norm_g),
        "out_norm_a_g": row3(out_norm_a_g), "out_norm_b_g": row3(out_norm_b_g),
        "norm_ffn_g": row3(norm_ffn_g),
        "conv_mix_wb": jnp.concatenate([conv_mix_w, conv_mix_b[:, None, :]], axis=1),
        "conv_ffn_wb": jnp.concatenate([conv_ffn_w, conv_ffn_b[:, None, :]], axis=1),
        "w_gate": w_gate, "w_up": w_up, "w_down": w_down,
        "w_s": w_s,
        "b_s_full": jnp.broadcast_to(b_s[..., None], (depth, N_HEADS, CHUNK, HEAD_DIM)),
        "w8": w_s[:, :, :dec_seq, :dec_seq].reshape(depth, -1),
        "b8": b_s[:, :, :dec_seq].reshape(depth, -1),
        "st_ffn": tmajor(state_conv_ffn),
        "final_norm_g": final_norm_g.reshape(1, D_MODEL),
    }
    st_mix = tmajor(state_conv_mix)

    xp = x_prompt.reshape(n_seq * seq, D_MODEL)
    xs = tmajor(x_sample)
    flat = lambda a: a.reshape(dec_seq * n_dec, D_MODEL)
    proj_b = (w_in[0].astype(BF16), w_out[0].astype(BF16))
    mix_p, ffn_p, mix_s, ffn_s, v_s = [], [], [], [], []
    for l in range(depth):
        final = l == depth - 1
        x1_s, h2_s, nm_s, v_rows = _mixer(xs, *proj_b, l, p, sample=True, st_in=st_mix)
        x2_s, nf_s, w_bf16 = _ffn(flat(h2_s), flat(x1_s), l, p, sample=True, final=final)
        x1_p, h2_p, nm_p = _mixer(xp, *proj_b, l, p, sample=False)
        cast_next = None if final else (w_in, w_out)
        xp, nf_p, proj_b = _ffn(h2_p, x1_p, l, p, sample=False, final=final, w_bf16=w_bf16,
                                cast_next=cast_next)
        xs = x2_s.reshape(dec_seq, n_dec, D_MODEL)
        mix_p.append(nm_p)
        ffn_p.append(nf_p)
        mix_s.append(nm_s)
        ffn_s.append(nf_s)
        v_s.append(v_rows)
    y_prompt = xp.reshape(n_seq, seq, D_MODEL)
    y_sample = tmajor(xs)
    return (y_prompt, y_sample, jnp.stack(mix_p), jnp.stack(ffn_p),
            tmajor(jnp.stack(mix_s)), tmajor(jnp.stack(ffn_s)), tmajor(jnp.stack(v_s)))
```

```python
import functools

import jax
import jax.numpy as jnp
from jax import lax
from jax.experimental import pallas as pl
from jax.experimental.pallas import tpu as pltpu

D_MODEL = 2048
A_WIDTH = 1024
B_WIDTH = 1024
HEAD_DIM = 128
N_HEADS = 8
CHUNK = 128
CONV_W = 3
D_FF = 5632
IN_COLS = 2 * A_WIDTH + 3 * B_WIDTH
EPS = 1e-6

VMEM_LIMIT_BYTES = 56 * 1024 * 1024
VMEM_LIMIT_MIXER_PROMPT_BYTES = 60 * 1024 * 1024

BM_MIX_PROMPT = 512
BM_MIX_SAMPLE = 256
BM_FFN = 1024
TF_PROMPT = 512
TF_SAMPLE = 256
TN = 512
RES_W = 512
N_RES = D_MODEL // RES_W
CAST_ROWS = 64
N_CAST = D_MODEL // CAST_ROWS

F32 = jnp.float32
BF16 = jnp.bfloat16


def _rms(x, g):
    ms = jnp.mean(x * x, axis=-1, keepdims=True)
    return x * lax.rsqrt(ms + EPS) * g


def _gelu(x):
    c = 0.7978845608028654
    return 0.5 * x * (1.0 + jnp.tanh(c * (x + 0.044715 * (x * x * x))))


def _group_norm(y, g):
    outs = []
    for hd in range(N_HEADS):
        blk = y[:, hd * HEAD_DIM:(hd + 1) * HEAD_DIM]
        ms = jnp.mean(blk * blk, axis=-1, keepdims=True)
        outs.append(blk * lax.rsqrt(ms + EPS))
    return jnp.concatenate(outs, axis=-1) * g


def _shift_rows_prompt(z, carry):
    r1 = pltpu.roll(z, 1, 0)
    r2 = pltpu.roll(z, 2, 0)
    c1 = pltpu.roll(carry, 1, 0)
    c2 = pltpu.roll(carry, 2, 0)
    rid = lax.broadcasted_iota(jnp.int32, carry.shape, 0)
    top1 = jnp.where(rid < 1, c1, r1[0:8])
    top2 = jnp.where(rid < 2, c2, r2[0:8])
    p1 = jnp.concatenate([top1, r1[8:]], axis=0)
    p2 = jnp.concatenate([top2, r2[8:]], axis=0)
    return p1, p2


def _shift_rows_sample(z, st, bs):
    p1 = jnp.concatenate([st[1], z[:7 * bs]], axis=0)
    p2 = jnp.concatenate([st[0], st[1], z[:6 * bs]], axis=0)
    return p1, p2


def _conv3(z, p1, p2, cwb):
    return cwb[3:4] + cwb[0:1] * p2 + cwb[1:2] * p1 + cwb[2:3] * z


def _row_spec(l, n):
    return pl.BlockSpec((None, 1, n), lambda i, l=l: (l, 0, 0))


def _mixer_kernel(*refs, sample, bm, tiles_per_seq):
    it = iter(refs)
    x_ref, g1_ref, w_in_ref, gv_ref = next(it), next(it), next(it), next(it)
    if sample:
        w8_ref, b8_ref, st_in_ref = next(it), next(it), next(it)
    else:
        ws_ref, bs_ref = next(it), next(it)
    cwb_ref, ga_ref, gb_ref, w_out_ref, g2_ref = (next(it) for _ in range(5))
    x1_ref, h2_ref, st_out_ref = next(it), next(it), next(it)
    if sample:
        v_out_ref = next(it)
    else:
        carry_ref = next(it)

    bs = bm // 8
    x = x_ref[...].reshape(bm, D_MODEL)
    h = _rms(x, g1_ref[...]).astype(BF16)

    def proj(c0):
        return jnp.dot(h, w_in_ref[:, c0:c0 + 1024], preferred_element_type=F32)

    v = _rms(_gelu(proj(A_WIDTH)), gv_ref[...])
    u = _gelu(proj(0))

    if sample:
        v_out_ref[...] = v.reshape(8, bs, A_WIDTH)
        heads = []
        for hd in range(N_HEADS):
            vh = [v[t * bs:(t + 1) * bs, hd * HEAD_DIM:(hd + 1) * HEAD_DIM] for t in range(8)]
            rows = []
            for i in range(8):
                acc = w8_ref[(hd * 8 + i) * 8] * vh[0]
                for j in range(1, i + 1):
                    acc = acc + w8_ref[(hd * 8 + i) * 8 + j] * vh[j]
                rows.append(acc + b8_ref[hd * 8 + i])
            heads.append(jnp.concatenate(rows, axis=0))
        mixed = jnp.concatenate(heads, axis=1)
    else:
        nch = bm // CHUNK
        vb = v.astype(BF16)
        row = lax.broadcasted_iota(jnp.int32, (CHUNK, CHUNK), 0)
        col = lax.broadcasted_iota(jnp.int32, (CHUNK, CHUNK), 1)
        heads = []
        for hd in range(N_HEADS):
            wm = jnp.where(row >= col, ws_ref[hd], 0.0).astype(BF16)
            vh = jnp.concatenate(
                [vb[c * CHUNK:(c + 1) * CHUNK, hd * HEAD_DIM:(hd + 1) * HEAD_DIM] for c in range(nch)], axis=1)
            mh = jnp.dot(wm, vh, preferred_element_type=F32)
            bias = bs_ref[hd]
            heads.append(jnp.concatenate(
                [mh[:, c * HEAD_DIM:(c + 1) * HEAD_DIM] + bias for c in range(nch)], axis=0))
        mixed = jnp.concatenate(heads, axis=1)
    y_a = _group_norm(u * mixed, ga_ref[...]).astype(BF16)

    z = proj(2 * A_WIDTH + B_WIDTH) * proj(2 * A_WIDTH + 2 * B_WIDTH)
    if sample:
        p1, p2 = _shift_rows_sample(z, st_in_ref[...], bs)
        st_out_ref[...] = z[6 * bs:].reshape(2, bs, B_WIDTH)
    else:
        first = (pl.program_id(0) % tiles_per_seq) == 0
        carry = jnp.where(first, 0.0, carry_ref[...])
        p1, p2 = _shift_rows_prompt(z, carry)
        carry_ref[...] = z[bm - 8:bm]
        st_out_ref[...] = z[bm - 2:bm]
    zc = _conv3(z, p1, p2, cwb_ref[...])
    y_b = _group_norm(proj(2 * A_WIDTH) * zc, gb_ref[...]).astype(BF16)

    x1 = (x + jnp.dot(y_a, w_out_ref[0:A_WIDTH, :], preferred_element_type=F32)
          + jnp.dot(y_b, w_out_ref[A_WIDTH:, :], preferred_element_type=F32))
    h2 = _rms(x1, g2_ref[...]).astype(BF16)
    x1_ref[...] = x1.reshape(x1_ref.shape)
    h2_ref[...] = h2.reshape(h2_ref.shape)


def _mixer(x, w_in_b, w_out_b, l, p, *, sample, st_in=None):
    rows = 8 * 128 if sample else x.shape[0]
    bm = BM_MIX_SAMPLE if sample else BM_MIX_PROMPT
    bs = bm // 8
    n_tiles = rows // bm
    if sample:
        x_spec = pl.BlockSpec((8, bs, D_MODEL), lambda i: (0, i, 0))
    else:
        x_spec = pl.BlockSpec((bm, D_MODEL), lambda i: (i, 0))
    args = [x, p["norm_mix_g"], w_in_b, p["sgu_norm_g"]]
    specs = [x_spec, _row_spec(l, D_MODEL),
             pl.BlockSpec((D_MODEL, IN_COLS), lambda i: (0, 0), pipeline_mode=pl.Buffered(1)),
             _row_spec(l, A_WIDTH)]
    if sample:
        args += [p["w8"][l], p["b8"][l], st_in]
        specs += [pl.BlockSpec(memory_space=pltpu.SMEM), pl.BlockSpec(memory_space=pltpu.SMEM),
                  pl.BlockSpec((None, 2, bs, B_WIDTH), lambda i: (l, 0, i, 0))]
    else:
        args += [p["w_s"], p["b_s_full"]]
        specs += [pl.BlockSpec((None, N_HEADS, CHUNK, CHUNK), lambda i: (l, 0, 0, 0)),
                  pl.BlockSpec((None, N_HEADS, CHUNK, HEAD_DIM), lambda i: (l, 0, 0, 0))]
    args += [p["conv_mix_wb"], p["out_norm_a_g"], p["out_norm_b_g"], w_out_b, p["norm_ffn_g"]]
    specs += [pl.BlockSpec((None, CONV_W + 1, B_WIDTH), lambda i: (l, 0, 0)),
              _row_spec(l, A_WIDTH), _row_spec(l, B_WIDTH),
              pl.BlockSpec((D_MODEL, D_MODEL), lambda i: (0, 0), pipeline_mode=pl.Buffered(1)),
              _row_spec(l, D_MODEL)]

    out_shape = [jax.ShapeDtypeStruct(x.shape, F32), jax.ShapeDtypeStruct(x.shape, BF16)]
    out_specs = [x_spec, x_spec]
    scratch = []
    if sample:
        out_shape += [jax.ShapeDtypeStruct((2, 128, B_WIDTH), F32),
                      jax.ShapeDtypeStruct((8, 128, A_WIDTH), F32)]
        out_specs += [pl.BlockSpec((2, bs, B_WIDTH), lambda i: (0, i, 0)),
                      pl.BlockSpec((8, bs, A_WIDTH), lambda i: (0, i, 0))]
        tiles_per_seq = 1
    else:
        tiles_per_seq = 2048 // bm
        out_shape += [jax.ShapeDtypeStruct((rows // 2048, 2, B_WIDTH), F32)]
        out_specs += [pl.BlockSpec((None, 2, B_WIDTH), lambda i: (i // tiles_per_seq, 0, 0))]
        scratch = [pltpu.VMEM((8, B_WIDTH), F32)]

    return pl.pallas_call(
        functools.partial(_mixer_kernel, sample=sample, bm=bm, tiles_per_seq=tiles_per_seq),
        grid=(n_tiles,),
        in_specs=specs,
        out_specs=out_specs,
        out_shape=out_shape,
        scratch_shapes=scratch,
        compiler_params=pltpu.CompilerParams(
            dimension_semantics=("arbitrary",),
            vmem_limit_bytes=VMEM_LIMIT_BYTES if sample else VMEM_LIMIT_MIXER_PROMPT_BYTES),
        name="mixer_sample" if sample else "mixer_prompt",
    )(*args)


def _ffn_kernel(*refs, sample, final, cast_next, tf, n_f, tiles_per_seq):
    it = iter(refs)
    h_ref, xr_ref = next(it), next(it)
    if sample:
        wg_ref, wu_ref = next(it), next(it)
    else:
        wgu_ref = next(it)
    wd_ref, cwb_ref = next(it), next(it)
    st_in_ref = next(it) if sample else None
    gf_ref = next(it) if final else None
    if cast_next:
        win_ref, wout_ref = next(it), next(it)
    o_ref, st_out_ref = next(it), next(it)
    if sample:
        wgu_out_ref, wd_out_ref = next(it), next(it)
    if cast_next:
        win_out_ref, wout_out_ref = next(it), next(it)
    carry_ref = None if sample else next(it)

    i = pl.program_id(0)
    f = pl.program_id(1)
    bm = BM_FFN
    c0 = pl.multiple_of(f * tf, tf)

    @pl.when(f == 0)
    def _():
        o_ref[...] = jnp.zeros_like(o_ref)

    @pl.when(f < N_RES)
    def _():
        r0 = pl.multiple_of(f * RES_W, RES_W)
        o_ref[:, pl.ds(r0, RES_W)] += xr_ref[...]

    if cast_next:
        win_out_ref[...] = win_ref[...].astype(BF16)
        wout_out_ref[...] = wout_ref[...].astype(BF16)

    h = h_ref[...]
    if sample:
        wg = wg_ref[...].astype(BF16)
        wu = wu_ref[...].astype(BF16)
        wd = wd_ref[...].astype(BF16)
        wgu_out_ref[:, 0:tf] = wg
        wgu_out_ref[:, tf:] = wu
        wd_out_ref[...] = wd
        g = jnp.dot(h, wg, preferred_element_type=F32)
        u = jnp.dot(h, wu, preferred_element_type=F32)
        p1, p2 = _shift_rows_sample(g, st_in_ref[...], 128)
        st_out_ref[...] = g[6 * 128:].reshape(2, 128, tf)
    else:
        wd = wd_ref[...]
        gu = jnp.dot(h, wgu_ref[...], preferred_element_type=F32)
        ts = TF_SAMPLE
        g = jnp.concatenate([gu[:, 2 * k * ts:(2 * k + 1) * ts] for k in range(tf // ts)], axis=1)
        u = jnp.concatenate([gu[:, (2 * k + 1) * ts:(2 * k + 2) * ts] for k in range(tf // ts)], axis=1)
        first = (i % tiles_per_seq) == 0
        carry = jnp.where(first, 0.0, carry_ref[f])
        p1, p2 = _shift_rows_prompt(g, carry)
        carry_ref[f] = g[bm - 8:bm]
        st_out_ref[:, pl.ds(c0, tf)] = g[bm - 2:bm]
    gc = _conv3(g, p1, p2, cwb_ref[:, pl.ds(c0, tf)])
    a = (gc / (1.0 + jnp.exp(-gc)) * u).astype(BF16)
    for n in range(D_MODEL // TN):
        o_ref[:, n * TN:(n + 1) * TN] += jnp.dot(
            a, wd[:, n * TN:(n + 1) * TN], preferred_element_type=F32)

    if final:
        @pl.when(f == n_f - 1)
        def _():
            o_ref[...] = _rms(o_ref[...], gf_ref[...])


def _ffn(h2, x1, l, p, *, sample, final, w_bf16=None, cast_next=None):
    rows = h2.shape[0]
    n_tiles = rows // BM_FFN
    tf = TF_SAMPLE if sample else TF_PROMPT
    n_f = D_FF // tf
    assert n_f >= N_RES
    if sample:
        w_args = [p["w_gate"], p["w_up"], p["w_down"]]
        w_specs = [pl.BlockSpec((None, D_MODEL, tf), lambda i, f: (l, 0, f)),
                   pl.BlockSpec((None, D_MODEL, tf), lambda i, f: (l, 0, f)),
                   pl.BlockSpec((None, tf, D_MODEL), lambda i, f: (l, f, 0))]
    else:
        w_args = list(w_bf16)
        w_specs = [pl.BlockSpec((D_MODEL, 2 * tf), lambda i, f: (0, f)),
                   pl.BlockSpec((tf, D_MODEL), lambda i, f: (f, 0))]
    args = [h2, x1] + w_args + [p["conv_ffn_wb"]]
    specs = [pl.BlockSpec((BM_FFN, D_MODEL), lambda i, f: (i, 0)),
             pl.BlockSpec((BM_FFN, RES_W), lambda i, f: (i, jnp.minimum(f, N_RES - 1)))] + w_specs + [
             pl.BlockSpec((None, CONV_W + 1, D_FF), lambda i, f: (l, 0, 0))]
    out_shape = [jax.ShapeDtypeStruct((rows, D_MODEL), F32)]
    out_specs = [pl.BlockSpec((BM_FFN, D_MODEL), lambda i, f: (i, 0))]
    scratch = []
    if sample:
        args += [p["st_ffn"]]
        specs += [pl.BlockSpec((None, 2, 128, tf), lambda i, f: (l, 0, 0, f))]
    if final:
        args += [p["final_norm_g"]]
        specs += [pl.BlockSpec((1, D_MODEL), lambda i, f: (0, 0))]
    if cast_next is not None:
        chunks_per_tile = N_CAST // n_tiles
        steps_per_chunk = -(-n_f // chunks_per_tile)
        chunk = lambda i, f: (l + 1, i * chunks_per_tile + f // steps_per_chunk, 0)
        args += list(cast_next)
        specs += [pl.BlockSpec((None, CAST_ROWS, IN_COLS), chunk),
                  pl.BlockSpec((None, CAST_ROWS, D_MODEL), chunk)]
    if sample:
        out_shape += [jax.ShapeDtypeStruct((2, 128, D_FF), F32),
                      jax.ShapeDtypeStruct((D_MODEL, 2 * D_FF), BF16),
                      jax.ShapeDtypeStruct((D_FF, D_MODEL), BF16)]
        out_specs += [pl.BlockSpec((2, 128, tf), lambda i, f: (0, 0, f)),
                      pl.BlockSpec((D_MODEL, 2 * tf), lambda i, f: (0, f)),
                      pl.BlockSpec((tf, D_MODEL), lambda i, f: (f, 0))]
        tiles_per_seq = 1
    else:
        tiles_per_seq = 2048 // BM_FFN
        out_shape += [jax.ShapeDtypeStruct((n_tiles, 2, D_FF), F32)]
        out_specs += [pl.BlockSpec((None, 2, D_FF), lambda i, f: (i, 0, 0))]
        scratch = [pltpu.VMEM((n_f, 8, tf), F32)]
    if cast_next is not None:
        out_shape += [jax.ShapeDtypeStruct((D_MODEL, IN_COLS), BF16),
                      jax.ShapeDtypeStruct((D_MODEL, D_MODEL), BF16)]
        out_specs += [pl.BlockSpec((CAST_ROWS, IN_COLS), lambda i, f: chunk(i, f)[1:]),
                      pl.BlockSpec((CAST_ROWS, D_MODEL), lambda i, f: chunk(i, f)[1:])]
    outs = pl.pallas_call(
        functools.partial(_ffn_kernel, sample=sample, final=final, cast_next=cast_next is not None,
                          tf=tf, n_f=n_f, tiles_per_seq=tiles_per_seq),
        grid=(n_tiles, n_f),
        in_specs=specs,
        out_specs=out_specs,
        out_shape=out_shape,
        scratch_shapes=scratch,
        compiler_params=pltpu.CompilerParams(
            dimension_semantics=("arbitrary", "arbitrary"), vmem_limit_bytes=VMEM_LIMIT_BYTES),
        name="ffn_sample" if sample else "ffn_prompt",
    )(*args)
    if sample:
        return outs[0], outs[1], outs[2:]
    return outs[0], outs[1][tiles_per_seq - 1::tiles_per_seq], outs[2:]


def kernel(x_prompt, x_sample, state_conv_mix, state_conv_ffn, norm_mix_g, w_in, sgu_norm_g, w_s, b_s,
           conv_mix_w, conv_mix_b, out_norm_a_g, out_norm_b_g, w_out, norm_ffn_g, w_gate, w_up,
           conv_ffn_w, conv_ffn_b, w_down, final_norm_g):
    depth = w_in.shape[0]
    n_seq, seq, _ = x_prompt.shape
    n_dec, dec_seq, _ = x_sample.shape
    assert (seq, n_dec, dec_seq) == (2048, 128, 8)

    row3 = lambda a: a.reshape(depth, 1, a.shape[-1])
    tmajor = lambda a: jnp.swapaxes(a, -3, -2)
    p = {
        "norm_mix_g": row3(norm_mix_g), "sgu_norm_g": row3(sgu_norm_g),
        "out_norm_a_g": row3(out_norm_a_g), "out_norm_b_g": row3(out_norm_b_g),
        "norm_ffn_g": row3(norm_ffn_g),
        "conv_mix_wb": jnp.concatenate([conv_mix_w, conv_mix_b[:, None, :]], axis=1),
        "conv_ffn_wb": jnp.concatenate([conv_ffn_w, conv_ffn_b[:, None, :]], axis=1),
        "w_gate": w_gate, "w_up": w_up, "w_down": w_down,
        "w_s": w_s,
        "b_s_full": jnp.broadcast_to(b_s[..., None], (depth, N_HEADS, CHUNK, HEAD_DIM)),
        "w8": w_s[:, :, :dec_seq, :dec_seq].reshape(depth, -1),
        "b8": b_s[:, :, :dec_seq].reshape(depth, -1),
        "st_ffn": tmajor(state_conv_ffn),
        "final_norm_g": final_norm_g.reshape(1, D_MODEL),
    }
    st_mix = tmajor(state_conv_mix)

    xp = x_prompt.reshape(n_seq * seq, D_MODEL)
    xs = tmajor(x_sample)
    flat = lambda a: a.reshape(dec_seq * n_dec, D_MODEL)
    proj_b = (w_in[0].astype(BF16), w_out[0].astype(BF16))
    mix_p, ffn_p, mix_s, ffn_s, v_s = [], [], [], [], []
    for l in range(depth):
        final = l == depth - 1
        x1_s, h2_s, nm_s, v_rows = _mixer(xs, *proj_b, l, p, sample=True, st_in=st_mix)
        x2_s, nf_s, w_bf16 = _ffn(flat(h2_s), flat(x1_s), l, p, sample=True, final=final)
        x1_p, h2_p, nm_p = _mixer(xp, *proj_b, l, p, sample=False)
        xp, nf_p, _ = _ffn(h2_p, x1_p, l, p, sample=False, final=final, w_bf16=w_bf16)
        if not final:
            proj_b = (w_in[l + 1].astype(BF16), w_out[l + 1].astype(BF16))
        xs = x2_s.reshape(dec_seq, n_dec, D_MODEL)
        mix_p.append(nm_p)
        ffn_p.append(nf_p)
        mix_s.append(nm_s)
        ffn_s.append(nf_s)
        v_s.append(v_rows)
    y_prompt = xp.reshape(n_seq, seq, D_MODEL)
    y_sample = tmajor(xs)
    return (y_prompt, y_sample, jnp.stack(mix_p), jnp.stack(ffn_p),
            tmajor(jnp.stack(mix_s)), tmajor(jnp.stack(ffn_s)), tmajor(jnp.stack(v_s)))
```

```python
import functools

import jax
import jax.numpy as jnp
from jax import lax
from jax.experimental import pallas as pl
from jax.experimental.pallas import tpu as pltpu

D_MODEL = 2048
A_WIDTH = 1024
B_WIDTH = 1024
HEAD_DIM = 128
N_HEADS = 8
CHUNK = 128
CONV_W = 3
D_FF = 5632
IN_COLS = 2 * A_WIDTH + 3 * B_WIDTH
EPS = 1e-6

VMEM_LIMIT_BYTES = 56 * 1024 * 1024
VMEM_LIMIT_MIXER_PROMPT_BYTES = 60 * 1024 * 1024

BM_MIX_PROMPT = 512
BM_MIX_SAMPLE = 256
MIX_SUBTILES = 2
BM_FFN = 1024
TF_PROMPT = 512
TF_SAMPLE = 256
TN = 512
RES_W = 256
N_RES = D_MODEL // RES_W
FFN_SUBTILES = 2

F32 = jnp.float32
BF16 = jnp.bfloat16


def _rms(x, g):
    ms = jnp.mean(x * x, axis=-1, keepdims=True)
    return x * lax.rsqrt(ms + EPS) * g


def _gelu(x):
    c = 0.7978845608028654
    return 0.5 * x * (1.0 + jnp.tanh(c * (x + 0.044715 * (x * x * x))))


def _group_norm(y, g):
    outs = []
    for hd in range(N_HEADS):
        blk = y[:, hd * HEAD_DIM:(hd + 1) * HEAD_DIM]
        ms = jnp.mean(blk * blk, axis=-1, keepdims=True)
        outs.append(blk * lax.rsqrt(ms + EPS))
    return jnp.concatenate(outs, axis=-1) * g


def _shift_rows_prompt(z, carry):
    r1 = pltpu.roll(z, 1, 0)
    r2 = pltpu.roll(z, 2, 0)
    c1 = pltpu.roll(carry, 1, 0)
    c2 = pltpu.roll(carry, 2, 0)
    rid = lax.broadcasted_iota(jnp.int32, carry.shape, 0)
    top1 = jnp.where(rid < 1, c1, r1[0:8])
    top2 = jnp.where(rid < 2, c2, r2[0:8])
    p1 = jnp.concatenate([top1, r1[8:]], axis=0)
    p2 = jnp.concatenate([top2, r2[8:]], axis=0)
    return p1, p2


def _shift_rows_sample(z, st, bs):
    p1 = jnp.concatenate([st[1], z[:7 * bs]], axis=0)
    p2 = jnp.concatenate([st[0], st[1], z[:6 * bs]], axis=0)
    return p1, p2


def _conv3(z, p1, p2, cwb):
    return cwb[3:4] + cwb[0:1] * p2 + cwb[1:2] * p1 + cwb[2:3] * z


def _row_spec(l, n):
    return pl.BlockSpec((None, 1, n), lambda i, l=l: (l, 0, 0))


def _mixer_kernel(*refs, sample, bm, tiles_per_seq):
    it = iter(refs)
    x_ref, g1_ref, w_in_ref, gv_ref = next(it), next(it), next(it), next(it)
    if sample:
        w8_ref, b8_ref, st_in_ref = next(it), next(it), next(it)
    else:
        ws_ref, bs_ref = next(it), next(it)
    cwb_ref, ga_ref, gb_ref, w_out_ref, g2_ref = (next(it) for _ in range(5))
    x1_ref, h2_ref, st_out_ref = next(it), next(it), next(it)
    if sample:
        v_out_ref = next(it)
    else:
        carry_ref = next(it)

    def in_proj(x):
        h = _rms(x, g1_ref[...]).astype(BF16)
        return lambda c0: jnp.dot(h, w_in_ref[:, c0:c0 + 1024], preferred_element_type=F32)

    def out_proj(x, y_a, y_b):
        x1 = (x + jnp.dot(y_a, w_out_ref[0:A_WIDTH, :], preferred_element_type=F32)
              + jnp.dot(y_b, w_out_ref[A_WIDTH:, :], preferred_element_type=F32))
        return x1, _rms(x1, g2_ref[...]).astype(BF16)

    if sample:
        bs = bm // 8
        x = x_ref[...].reshape(bm, D_MODEL)
        proj = in_proj(x)
        v = _rms(_gelu(proj(A_WIDTH)), gv_ref[...])
        u = _gelu(proj(0))
        v_out_ref[...] = v.reshape(8, bs, A_WIDTH)
        heads = []
        for hd in range(N_HEADS):
            vh = [v[t * bs:(t + 1) * bs, hd * HEAD_DIM:(hd + 1) * HEAD_DIM] for t in range(8)]
            rows = []
            for i in range(8):
                acc = w8_ref[(hd * 8 + i) * 8] * vh[0]
                for j in range(1, i + 1):
                    acc = acc + w8_ref[(hd * 8 + i) * 8 + j] * vh[j]
                rows.append(acc + b8_ref[hd * 8 + i])
            heads.append(jnp.concatenate(rows, axis=0))
        mixed = jnp.concatenate(heads, axis=1)
        y_a = _group_norm(u * mixed, ga_ref[...]).astype(BF16)
        z = proj(2 * A_WIDTH + B_WIDTH) * proj(2 * A_WIDTH + 2 * B_WIDTH)
        p1, p2 = _shift_rows_sample(z, st_in_ref[...], bs)
        st_out_ref[...] = z[6 * bs:].reshape(2, bs, B_WIDTH)
        zc = _conv3(z, p1, p2, cwb_ref[...])
        y_b = _group_norm(proj(2 * A_WIDTH) * zc, gb_ref[...]).astype(BF16)
        x1, h2 = out_proj(x, y_a, y_b)
        x1_ref[...] = x1.reshape(x1_ref.shape)
        h2_ref[...] = h2.reshape(h2_ref.shape)
        return

    m = bm // MIX_SUBTILES
    nch = m // CHUNK
    row = lax.broadcasted_iota(jnp.int32, (CHUNK, CHUNK), 0)
    col = lax.broadcasted_iota(jnp.int32, (CHUNK, CHUNK), 1)
    wm = [jnp.where(row >= col, ws_ref[hd], 0.0).astype(BF16) for hd in range(N_HEADS)]
    first = (pl.program_id(0) % tiles_per_seq) == 0
    carry = jnp.where(first, 0.0, carry_ref[...])
    for s in range(MIX_SUBTILES):
        x = x_ref[s * m:(s + 1) * m, :]
        proj = in_proj(x)
        v = _rms(_gelu(proj(A_WIDTH)), gv_ref[...])
        u = _gelu(proj(0))
        vb = v.astype(BF16)
        heads = []
        for hd in range(N_HEADS):
            vh = jnp.concatenate(
                [vb[c * CHUNK:(c + 1) * CHUNK, hd * HEAD_DIM:(hd + 1) * HEAD_DIM] for c in range(nch)], axis=1)
            mh = jnp.dot(wm[hd], vh, preferred_element_type=F32)
            bias = bs_ref[hd]
            heads.append(jnp.concatenate(
                [mh[:, c * HEAD_DIM:(c + 1) * HEAD_DIM] + bias for c in range(nch)], axis=0))
        mixed = jnp.concatenate(heads, axis=1)
        y_a = _group_norm(u * mixed, ga_ref[...]).astype(BF16)
        z = proj(2 * A_WIDTH + B_WIDTH) * proj(2 * A_WIDTH + 2 * B_WIDTH)
        p1, p2 = _shift_rows_prompt(z, carry)
        carry = z[m - 8:m]
        zc = _conv3(z, p1, p2, cwb_ref[...])
        y_b = _group_norm(proj(2 * A_WIDTH) * zc, gb_ref[...]).astype(BF16)
        x1, h2 = out_proj(x, y_a, y_b)
        x1_ref[s * m:(s + 1) * m, :] = x1
        h2_ref[s * m:(s + 1) * m, :] = h2
    carry_ref[...] = carry
    st_out_ref[...] = carry[6:8]


def _mixer(x, w_in_b, w_out_b, l, p, *, sample, st_in=None):
    rows = 8 * 128 if sample else x.shape[0]
    bm = BM_MIX_SAMPLE if sample else BM_MIX_PROMPT
    bs = bm // 8
    n_tiles = rows // bm
    if sample:
        x_spec = pl.BlockSpec((8, bs, D_MODEL), lambda i: (0, i, 0))
    else:
        x_spec = pl.BlockSpec((bm, D_MODEL), lambda i: (i, 0))
    args = [x, p["norm_mix_g"], w_in_b, p["sgu_norm_g"]]
    specs = [x_spec, _row_spec(l, D_MODEL),
             pl.BlockSpec((None, D_MODEL, IN_COLS), lambda i: (l, 0, 0), pipeline_mode=pl.Buffered(1)),
             _row_spec(l, A_WIDTH)]
    if sample:
        args += [p["w8"][l], p["b8"][l], st_in]
        specs += [pl.BlockSpec(memory_space=pltpu.SMEM), pl.BlockSpec(memory_space=pltpu.SMEM),
                  pl.BlockSpec((None, 2, bs, B_WIDTH), lambda i: (l, 0, i, 0))]
    else:
        args += [p["w_s"], p["b_s_full"]]
        specs += [pl.BlockSpec((None, N_HEADS, CHUNK, CHUNK), lambda i: (l, 0, 0, 0)),
                  pl.BlockSpec((None, N_HEADS, CHUNK, HEAD_DIM), lambda i: (l, 0, 0, 0))]
    args += [p["conv_mix_wb"], p["out_norm_a_g"], p["out_norm_b_g"], w_out_b, p["norm_ffn_g"]]
    specs += [pl.BlockSpec((None, CONV_W + 1, B_WIDTH), lambda i: (l, 0, 0)),
              _row_spec(l, A_WIDTH), _row_spec(l, B_WIDTH),
              pl.BlockSpec((None, D_MODEL, D_MODEL), lambda i: (l, 0, 0), pipeline_mode=pl.Buffered(1)),
              _row_spec(l, D_MODEL)]

    out_shape = [jax.ShapeDtypeStruct(x.shape, F32), jax.ShapeDtypeStruct(x.shape, BF16)]
    out_specs = [x_spec, x_spec]
    scratch = []
    if sample:
        out_shape += [jax.ShapeDtypeStruct((2, 128, B_WIDTH), F32),
                      jax.ShapeDtypeStruct((8, 128, A_WIDTH), F32)]
        out_specs += [pl.BlockSpec((2, bs, B_WIDTH), lambda i: (0, i, 0)),
                      pl.BlockSpec((8, bs, A_WIDTH), lambda i: (0, i, 0))]
        tiles_per_seq = 1
    else:
        tiles_per_seq = 2048 // bm
        out_shape += [jax.ShapeDtypeStruct((rows // 2048, 2, B_WIDTH), F32)]
        out_specs += [pl.BlockSpec((None, 2, B_WIDTH), lambda i: (i // tiles_per_seq, 0, 0))]
        scratch = [pltpu.VMEM((8, B_WIDTH), F32)]

    return pl.pallas_call(
        functools.partial(_mixer_kernel, sample=sample, bm=bm, tiles_per_seq=tiles_per_seq),
        grid=(n_tiles,),
        in_specs=specs,
        out_specs=out_specs,
        out_shape=out_shape,
        scratch_shapes=scratch,
        compiler_params=pltpu.CompilerParams(
            dimension_semantics=("arbitrary",),
            vmem_limit_bytes=VMEM_LIMIT_BYTES if sample else VMEM_LIMIT_MIXER_PROMPT_BYTES),
        name="mixer_sample" if sample else "mixer_prompt",
    )(*args)


def _ffn_kernel(*refs, sample, final, tf, n_f, tiles_per_seq):
    it = iter(refs)
    h_ref, xr_ref, wg_ref, wu_ref, wd_ref, cwb_ref = (next(it) for _ in range(6))
    st_in_ref = next(it) if sample else None
    gf_ref = next(it) if final else None
    o_ref, st_out_ref = next(it), next(it)
    if sample:
        wg_out_ref, wu_out_ref, wd_out_ref = next(it), next(it), next(it)
    else:
        carry_ref = next(it)

    i = pl.program_id(0)
    f = pl.program_id(1)

    @pl.when(f == 0)
    def _():
        o_ref[...] = jnp.zeros_like(o_ref)

    @pl.when(f < N_RES)
    def _():
        r0 = pl.multiple_of(f * RES_W, RES_W)
        o_ref[:, pl.ds(r0, RES_W)] += xr_ref[...]

    cwb = cwb_ref[...]

    def act_down(r0, m, g, u, p1, p2, wd):
        gc = _conv3(g, p1, p2, cwb)
        a = (gc / (1.0 + jnp.exp(-gc)) * u).astype(BF16)
        for n in range(D_MODEL // TN):
            o_ref[r0:r0 + m, n * TN:(n + 1) * TN] += jnp.dot(
                a, wd[:, n * TN:(n + 1) * TN], preferred_element_type=F32)

    if sample:
        wg = wg_ref[...].astype(BF16)
        wu = wu_ref[...].astype(BF16)
        wd = wd_ref[...].astype(BF16)
        wg_out_ref[...] = wg
        wu_out_ref[...] = wu
        wd_out_ref[...] = wd
        h = h_ref[...]
        g = jnp.dot(h, wg, preferred_element_type=F32)
        u = jnp.dot(h, wu, preferred_element_type=F32)
        p1, p2 = _shift_rows_sample(g, st_in_ref[...], 128)
        st_out_ref[...] = g[6 * 128:].reshape(2, 128, tf)
        act_down(0, BM_FFN, g, u, p1, p2, wd)
    else:
        wg, wu, wd = wg_ref[...], wu_ref[...], wd_ref[...]
        first = (i % tiles_per_seq) == 0
        carry = jnp.where(first, 0.0, carry_ref[f])
        m = BM_FFN // FFN_SUBTILES
        for s in range(FFN_SUBTILES):
            h = h_ref[s * m:(s + 1) * m, :]
            g = jnp.dot(h, wg, preferred_element_type=F32)
            u = jnp.dot(h, wu, preferred_element_type=F32)
            p1, p2 = _shift_rows_prompt(g, carry)
            carry = g[m - 8:m]
            act_down(s * m, m, g, u, p1, p2, wd)
        carry_ref[f] = carry
        st_out_ref[...] = carry[6:8]

    if final:
        @pl.when(f == n_f - 1)
        def _():
            o_ref[...] = _rms(o_ref[...], gf_ref[...])


def _ffn(h2, x1, l, p, *, sample, final, w_bf16=None):
    rows = h2.shape[0]
    n_tiles = rows // BM_FFN
    tf = TF_SAMPLE if sample else TF_PROMPT
    n_f = D_FF // tf
    assert n_f >= N_RES
    if sample:
        w_args = [p["w_gate"], p["w_up"], p["w_down"]]
        w_specs = [pl.BlockSpec((None, D_MODEL, tf), lambda i, f: (l, 0, f)),
                   pl.BlockSpec((None, D_MODEL, tf), lambda i, f: (l, 0, f)),
                   pl.BlockSpec((None, tf, D_MODEL), lambda i, f: (l, f, 0))]
    else:
        w_args = list(w_bf16)
        w_specs = [pl.BlockSpec((D_MODEL, tf), lambda i, f: (0, f)),
                   pl.BlockSpec((D_MODEL, tf), lambda i, f: (0, f)),
                   pl.BlockSpec((tf, D_MODEL), lambda i, f: (f, 0))]
    args = [h2, x1] + w_args + [p["conv_ffn_wb"]]
    specs = [pl.BlockSpec((BM_FFN, D_MODEL), lambda i, f: (i, 0)),
             pl.BlockSpec((BM_FFN, RES_W), lambda i, f: (i, jnp.minimum(f, N_RES - 1)))] + w_specs + [
             pl.BlockSpec((None, CONV_W + 1, tf), lambda i, f: (l, 0, f))]
    out_shape = [jax.ShapeDtypeStruct((rows, D_MODEL), F32)]
    out_specs = [pl.BlockSpec((BM_FFN, D_MODEL), lambda i, f: (i, 0))]
    scratch = []
    if sample:
        args += [p["st_ffn"]]
        specs += [pl.BlockSpec((None, 2, 128, tf), lambda i, f: (l, 0, 0, f))]
    if final:
        args += [p["final_norm_g"]]
        specs += [pl.BlockSpec((1, D_MODEL), lambda i, f: (0, 0))]
    if sample:
        out_shape += [jax.ShapeDtypeStruct((2, 128, D_FF), F32),
                      jax.ShapeDtypeStruct((D_MODEL, D_FF), BF16),
                      jax.ShapeDtypeStruct((D_MODEL, D_FF), BF16),
                      jax.ShapeDtypeStruct((D_FF, D_MODEL), BF16)]
        out_specs += [pl.BlockSpec((2, 128, tf), lambda i, f: (0, 0, f)),
                      pl.BlockSpec((D_MODEL, tf), lambda i, f: (0, f)),
                      pl.BlockSpec((D_MODEL, tf), lambda i, f: (0, f)),
                      pl.BlockSpec((tf, D_MODEL), lambda i, f: (f, 0))]
        tiles_per_seq = 1
    else:
        tiles_per_seq = 2048 // BM_FFN
        out_shape += [jax.ShapeDtypeStruct((n_tiles, 2, D_FF), F32)]
        out_specs += [pl.BlockSpec((None, 2, tf), lambda i, f: (i, 0, f))]
        scratch = [pltpu.VMEM((n_f, 8, tf), F32)]
    outs = pl.pallas_call(
        functools.partial(_ffn_kernel, sample=sample, final=final,
                          tf=tf, n_f=n_f, tiles_per_seq=tiles_per_seq),
        grid=(n_tiles, n_f),
        in_specs=specs,
        out_specs=out_specs,
        out_shape=out_shape,
        scratch_shapes=scratch,
        compiler_params=pltpu.CompilerParams(
            dimension_semantics=("arbitrary", "arbitrary"), vmem_limit_bytes=VMEM_LIMIT_BYTES),
        name="ffn_sample" if sample else "ffn_prompt",
    )(*args)
    if sample:
        return outs[0], outs[1], outs[2:]
    return outs[0], outs[1][tiles_per_seq - 1::tiles_per_seq]


def kernel(x_prompt, x_sample, state_conv_mix, state_conv_ffn, norm_mix_g, w_in, sgu_norm_g, w_s, b_s,
           conv_mix_w, conv_mix_b, out_norm_a_g, out_norm_b_g, w_out, norm_ffn_g, w_gate, w_up,
           conv_ffn_w, conv_ffn_b, w_down, final_norm_g):
    depth = w_in.shape[0]
    n_seq, seq, _ = x_prompt.shape
    n_dec, dec_seq, _ = x_sample.shape
    assert (seq, n_dec, dec_seq) == (2048, 128, 8)

    row3 = lambda a: a.reshape(depth, 1, a.shape[-1])
    tmajor = lambda a: jnp.swapaxes(a, -3, -2)
    p = {
        "norm_mix_g": row3(norm_mix_g), "sgu_norm_g": row3(sgu_norm_g),
        "out_norm_a_g": row3(out_norm_a_g), "out_norm_b_g": row3(out_norm_b_g),
        "norm_ffn_g": row3(norm_ffn_g),
        "conv_mix_wb": jnp.concatenate([conv_mix_w, conv_mix_b[:, None, :]], axis=1),
        "conv_ffn_wb": jnp.concatenate([conv_ffn_w, conv_ffn_b[:, None, :]], axis=1),
        "w_gate": w_gate, "w_up": w_up, "w_down": w_down,
        "w_s": w_s,
        "b_s_full": jnp.broadcast_to(b_s[..., None], (depth, N_HEADS, CHUNK, HEAD_DIM)),
        "w8": w_s[:, :, :dec_seq, :dec_seq].reshape(depth, -1),
        "b8": b_s[:, :, :dec_seq].reshape(depth, -1),
        "st_ffn": tmajor(state_conv_ffn),
        "final_norm_g": final_norm_g.reshape(1, D_MODEL),
    }
    st_mix = tmajor(state_conv_mix)

    xp = x_prompt.reshape(n_seq * seq, D_MODEL)
    xs = tmajor(x_sample)
    flat = lambda a: a.reshape(dec_seq * n_dec, D_MODEL)
    w_in_b, w_out_b = w_in.astype(BF16), w_out.astype(BF16)
    mix_p, ffn_p, mix_s, ffn_s, v_s = [], [], [], [], []
    for l in range(depth):
        final = l == depth - 1
        x1_s, h2_s, nm_s, v_rows = _mixer(xs, w_in_b, w_out_b, l, p, sample=True, st_in=st_mix)
        x2_s, nf_s, w_bf16 = _ffn(flat(h2_s), flat(x1_s), l, p, sample=True, final=final)
        x1_p, h2_p, nm_p = _mixer(xp, w_in_b, w_out_b, l, p, sample=False)
        xp, nf_p = _ffn(h2_p, x1_p, l, p, sample=False, final=final, w_bf16=w_bf16)
        xs = x2_s.reshape(dec_seq, n_dec, D_MODEL)
        mix_p.append(nm_p)
        ffn_p.append(nf_p)
        mix_s.append(nm_s)
        ffn_s.append(nf_s)
        v_s.append(v_rows)
    y_prompt = xp.reshape(n_seq, seq, D_MODEL)
    y_sample = tmajor(xs)
    return (y_prompt, y_sample, jnp.stack(mix_p), jnp.stack(ffn_p),
            tmajor(jnp.stack(mix_s)), tmajor(jnp.stack(ffn_s)), tmajor(jnp.stack(v_s)))
```

```python
import functools

import jax
import jax.numpy as jnp
from jax import lax
from jax.experimental import pallas as pl
from jax.experimental.pallas import tpu as pltpu

D_MODEL = 2048
A_WIDTH = 1024
B_WIDTH = 1024
HEAD_DIM = 128
N_HEADS = 8
CHUNK = 128
CONV_W = 3
D_FF = 5632
IN_COLS = 2 * A_WIDTH + 3 * B_WIDTH
EPS = 1e-6

VMEM_LIMIT_BYTES = 56 * 1024 * 1024
VMEM_LIMIT_MIXER_PROMPT_BYTES = 60 * 1024 * 1024

BM_MIX_PROMPT = 512
BM_MIX_SAMPLE = 256
MIX_SUBTILES = 2
BM_FFN = 1024
TF_PROMPT = 512
TF_SAMPLE = 256
TN = 512
RES_W = 256
N_RES = D_MODEL // RES_W
FFN_SUBTILES = 2
CAST_ROWS = 64
N_CAST = D_MODEL // CAST_ROWS

F32 = jnp.float32
BF16 = jnp.bfloat16


def _rms(x, g):
    ms = jnp.mean(x * x, axis=-1, keepdims=True)
    return x * lax.rsqrt(ms + EPS) * g


def _gelu(x):
    c = 0.7978845608028654
    return 0.5 * x * (1.0 + jnp.tanh(c * (x + 0.044715 * (x * x * x))))


def _group_norm(y, g):
    outs = []
    for hd in range(N_HEADS):
        blk = y[:, hd * HEAD_DIM:(hd + 1) * HEAD_DIM]
        ms = jnp.mean(blk * blk, axis=-1, keepdims=True)
        outs.append(blk * lax.rsqrt(ms + EPS))
    return jnp.concatenate(outs, axis=-1) * g


def _shift_rows_prompt(z, carry):
    r1 = pltpu.roll(z, 1, 0)
    r2 = pltpu.roll(z, 2, 0)
    c1 = pltpu.roll(carry, 1, 0)
    c2 = pltpu.roll(carry, 2, 0)
    rid = lax.broadcasted_iota(jnp.int32, carry.shape, 0)
    top1 = jnp.where(rid < 1, c1, r1[0:8])
    top2 = jnp.where(rid < 2, c2, r2[0:8])
    p1 = jnp.concatenate([top1, r1[8:]], axis=0)
    p2 = jnp.concatenate([top2, r2[8:]], axis=0)
    return p1, p2


def _shift_rows_sample(z, st, bs):
    p1 = jnp.concatenate([st[1], z[:7 * bs]], axis=0)
    p2 = jnp.concatenate([st[0], st[1], z[:6 * bs]], axis=0)
    return p1, p2


def _conv3(z, p1, p2, cwb):
    return cwb[3:4] + cwb[0:1] * p2 + cwb[1:2] * p1 + cwb[2:3] * z


def _row_spec(l, n):
    return pl.BlockSpec((None, 1, n), lambda i, l=l: (l, 0, 0))


def _mixer_kernel(*refs, sample, bm, tiles_per_seq):
    it = iter(refs)
    x_ref, g1_ref, w_in_ref, gv_ref = next(it), next(it), next(it), next(it)
    if sample:
        w8_ref, b8_ref, st_in_ref = next(it), next(it), next(it)
    else:
        ws_ref, bs_ref = next(it), next(it)
    cwb_ref, ga_ref, gb_ref, w_out_ref, g2_ref = (next(it) for _ in range(5))
    x1_ref, h2_ref, st_out_ref = next(it), next(it), next(it)
    if sample:
        v_out_ref = next(it)
    else:
        carry_ref = next(it)

    def in_proj(x):
        h = _rms(x, g1_ref[...]).astype(BF16)
        return lambda c0: jnp.dot(h, w_in_ref[:, c0:c0 + 1024], preferred_element_type=F32)

    def out_proj(x, y_a, y_b):
        x1 = (x + jnp.dot(y_a, w_out_ref[0:A_WIDTH, :], preferred_element_type=F32)
              + jnp.dot(y_b, w_out_ref[A_WIDTH:, :], preferred_element_type=F32))
        return x1, _rms(x1, g2_ref[...]).astype(BF16)

    if sample:
        bs = bm // 8
        x = x_ref[...].reshape(bm, D_MODEL)
        proj = in_proj(x)
        v = _rms(_gelu(proj(A_WIDTH)), gv_ref[...])
        u = _gelu(proj(0))
        v_out_ref[...] = v.reshape(8, bs, A_WIDTH)
        heads = []
        for hd in range(N_HEADS):
            vh = [v[t * bs:(t + 1) * bs, hd * HEAD_DIM:(hd + 1) * HEAD_DIM] for t in range(8)]
            rows = []
            for i in range(8):
                acc = w8_ref[(hd * 8 + i) * 8] * vh[0]
                for j in range(1, i + 1):
                    acc = acc + w8_ref[(hd * 8 + i) * 8 + j] * vh[j]
                rows.append(acc + b8_ref[hd * 8 + i])
            heads.append(jnp.concatenate(rows, axis=0))
        mixed = jnp.concatenate(heads, axis=1)
        y_a = _group_norm(u * mixed, ga_ref[...]).astype(BF16)
        z = proj(2 * A_WIDTH + B_WIDTH) * proj(2 * A_WIDTH + 2 * B_WIDTH)
        p1, p2 = _shift_rows_sample(z, st_in_ref[...], bs)
        st_out_ref[...] = z[6 * bs:].reshape(2, bs, B_WIDTH)
        zc = _conv3(z, p1, p2, cwb_ref[...])
        y_b = _group_norm(proj(2 * A_WIDTH) * zc, gb_ref[...]).astype(BF16)
        x1, h2 = out_proj(x, y_a, y_b)
        x1_ref[...] = x1.reshape(x1_ref.shape)
        h2_ref[...] = h2.reshape(h2_ref.shape)
        return

    m = bm // MIX_SUBTILES
    nch = m // CHUNK
    row = lax.broadcasted_iota(jnp.int32, (CHUNK, CHUNK), 0)
    col = lax.broadcasted_iota(jnp.int32, (CHUNK, CHUNK), 1)
    wm = [jnp.where(row >= col, ws_ref[hd], 0.0).astype(BF16) for hd in range(N_HEADS)]
    first = (pl.program_id(0) % tiles_per_seq) == 0
    carry = jnp.where(first, 0.0, carry_ref[...])
    for s in range(MIX_SUBTILES):
        x = x_ref[s * m:(s + 1) * m, :]
        proj = in_proj(x)
        v = _rms(_gelu(proj(A_WIDTH)), gv_ref[...])
        u = _gelu(proj(0))
        vb = v.astype(BF16)
        heads = []
        for hd in range(N_HEADS):
            vh = jnp.concatenate(
                [vb[c * CHUNK:(c + 1) * CHUNK, hd * HEAD_DIM:(hd + 1) * HEAD_DIM] for c in range(nch)], axis=1)
            mh = jnp.dot(wm[hd], vh, preferred_element_type=F32)
            bias = bs_ref[hd]
            heads.append(jnp.concatenate(
                [mh[:, c * HEAD_DIM:(c + 1) * HEAD_DIM] + bias for c in range(nch)], axis=0))
        mixed = jnp.concatenate(heads, axis=1)
        y_a = _group_norm(u * mixed, ga_ref[...]).astype(BF16)
        z = proj(2 * A_WIDTH + B_WIDTH) * proj(2 * A_WIDTH + 2 * B_WIDTH)
        p1, p2 = _shift_rows_prompt(z, carry)
        carry = z[m - 8:m]
        zc = _conv3(z, p1, p2, cwb_ref[...])
        y_b = _group_norm(proj(2 * A_WIDTH) * zc, gb_ref[...]).astype(BF16)
        x1, h2 = out_proj(x, y_a, y_b)
        x1_ref[s * m:(s + 1) * m, :] = x1
        h2_ref[s * m:(s + 1) * m, :] = h2
    carry_ref[...] = carry
    st_out_ref[...] = carry[6:8]


def _mixer(x, w_in_b, w_out_b, l, p, *, sample, st_in=None):
    rows = 8 * 128 if sample else x.shape[0]
    bm = BM_MIX_SAMPLE if sample else BM_MIX_PROMPT
    bs = bm // 8
    n_tiles = rows // bm
    if sample:
        x_spec = pl.BlockSpec((8, bs, D_MODEL), lambda i: (0, i, 0))
    else:
        x_spec = pl.BlockSpec((bm, D_MODEL), lambda i: (i, 0))
    args = [x, p["norm_mix_g"], w_in_b, p["sgu_norm_g"]]
    specs = [x_spec, _row_spec(l, D_MODEL),
             pl.BlockSpec((D_MODEL, IN_COLS), lambda i: (0, 0), pipeline_mode=pl.Buffered(1)),
             _row_spec(l, A_WIDTH)]
    if sample:
        args += [p["w8"][l], p["b8"][l], st_in]
        specs += [pl.BlockSpec(memory_space=pltpu.SMEM), pl.BlockSpec(memory_space=pltpu.SMEM),
                  pl.BlockSpec((None, 2, bs, B_WIDTH), lambda i: (l, 0, i, 0))]
    else:
        args += [p["w_s"], p["b_s_full"]]
        specs += [pl.BlockSpec((None, N_HEADS, CHUNK, CHUNK), lambda i: (l, 0, 0, 0)),
                  pl.BlockSpec((None, N_HEADS, CHUNK, HEAD_DIM), lambda i: (l, 0, 0, 0))]
    args += [p["conv_mix_wb"], p["out_norm_a_g"], p["out_norm_b_g"], w_out_b, p["norm_ffn_g"]]
    specs += [pl.BlockSpec((None, CONV_W + 1, B_WIDTH), lambda i: (l, 0, 0)),
              _row_spec(l, A_WIDTH), _row_spec(l, B_WIDTH),
              pl.BlockSpec((D_MODEL, D_MODEL), lambda i: (0, 0), pipeline_mode=pl.Buffered(1)),
              _row_spec(l, D_MODEL)]

    out_shape = [jax.ShapeDtypeStruct(x.shape, F32), jax.ShapeDtypeStruct(x.shape, BF16)]
    out_specs = [x_spec, x_spec]
    scratch = []
    if sample:
        out_shape += [jax.ShapeDtypeStruct((2, 128, B_WIDTH), F32),
                      jax.ShapeDtypeStruct((8, 128, A_WIDTH), F32)]
        out_specs += [pl.BlockSpec((2, bs, B_WIDTH), lambda i: (0, i, 0)),
                      pl.BlockSpec((8, bs, A_WIDTH), lambda i: (0, i, 0))]
        tiles_per_seq = 1
    else:
        tiles_per_seq = 2048 // bm
        out_shape += [jax.ShapeDtypeStruct((rows // 2048, 2, B_WIDTH), F32)]
        out_specs += [pl.BlockSpec((None, 2, B_WIDTH), lambda i: (i // tiles_per_seq, 0, 0))]
        scratch = [pltpu.VMEM((8, B_WIDTH), F32)]

    return pl.pallas_call(
        functools.partial(_mixer_kernel, sample=sample, bm=bm, tiles_per_seq=tiles_per_seq),
        grid=(n_tiles,),
        in_specs=specs,
        out_specs=out_specs,
        out_shape=out_shape,
        scratch_shapes=scratch,
        compiler_params=pltpu.CompilerParams(
            dimension_semantics=("arbitrary",),
            vmem_limit_bytes=VMEM_LIMIT_BYTES if sample else VMEM_LIMIT_MIXER_PROMPT_BYTES),
        name="mixer_sample" if sample else "mixer_prompt",
    )(*args)


def _ffn_kernel(*refs, sample, final, cast_next, tf, n_f, tiles_per_seq):
    it = iter(refs)
    h_ref, xr_ref, wg_ref, wu_ref, wd_ref, cwb_ref = (next(it) for _ in range(6))
    st_in_ref = next(it) if sample else None
    gf_ref = next(it) if final else None
    if cast_next:
        win_ref, wout_ref = next(it), next(it)
    o_ref, st_out_ref = next(it), next(it)
    if sample:
        wg_out_ref, wu_out_ref, wd_out_ref = next(it), next(it), next(it)
    if cast_next:
        win_out_ref, wout_out_ref = next(it), next(it)
    if not sample:
        carry_ref = next(it)

    i = pl.program_id(0)
    f = pl.program_id(1)

    @pl.when(f == 0)
    def _():
        o_ref[...] = jnp.zeros_like(o_ref)

    @pl.when(f < N_RES)
    def _():
        r0 = pl.multiple_of(f * RES_W, RES_W)
        o_ref[:, pl.ds(r0, RES_W)] += xr_ref[...]

    cwb = cwb_ref[...]
    if cast_next:
        win_out_ref[...] = win_ref[...].astype(BF16)
        wout_out_ref[...] = wout_ref[...].astype(BF16)

    def act_down(r0, m, g, u, p1, p2, wd):
        gc = _conv3(g, p1, p2, cwb)
        a = (gc / (1.0 + jnp.exp(-gc)) * u).astype(BF16)
        for n in range(D_MODEL // TN):
            o_ref[r0:r0 + m, n * TN:(n + 1) * TN] += jnp.dot(
                a, wd[:, n * TN:(n + 1) * TN], preferred_element_type=F32)

    if sample:
        wg = wg_ref[...].astype(BF16)
        wu = wu_ref[...].astype(BF16)
        wd = wd_ref[...].astype(BF16)
        wg_out_ref[...] = wg
        wu_out_ref[...] = wu
        wd_out_ref[...] = wd
        h = h_ref[...]
        g = jnp.dot(h, wg, preferred_element_type=F32)
        u = jnp.dot(h, wu, preferred_element_type=F32)
        p1, p2 = _shift_rows_sample(g, st_in_ref[...], 128)
        st_out_ref[...] = g[6 * 128:].reshape(2, 128, tf)
        act_down(0, BM_FFN, g, u, p1, p2, wd)
    else:
        wg, wu, wd = wg_ref[...], wu_ref[...], wd_ref[...]
        first = (i % tiles_per_seq) == 0
        carry = jnp.where(first, 0.0, carry_ref[f])
        m = BM_FFN // FFN_SUBTILES
        for s in range(FFN_SUBTILES):
            h = h_ref[s * m:(s + 1) * m, :]
            g = jnp.dot(h, wg, preferred_element_type=F32)
            u = jnp.dot(h, wu, preferred_element_type=F32)
            p1, p2 = _shift_rows_prompt(g, carry)
            carry = g[m - 8:m]
            act_down(s * m, m, g, u, p1, p2, wd)
        carry_ref[f] = carry
        st_out_ref[...] = carry[6:8]

    if final:
        @pl.when(f == n_f - 1)
        def _():
            o_ref[...] = _rms(o_ref[...], gf_ref[...])


def _ffn(h2, x1, l, p, *, sample, final, w_bf16=None, cast_next=None):
    rows = h2.shape[0]
    n_tiles = rows // BM_FFN
    tf = TF_SAMPLE if sample else TF_PROMPT
    n_f = D_FF // tf
    assert n_f >= N_RES
    if sample:
        w_args = [p["w_gate"], p["w_up"], p["w_down"]]
        w_specs = [pl.BlockSpec((None, D_MODEL, tf), lambda i, f: (l, 0, f)),
                   pl.BlockSpec((None, D_MODEL, tf), lambda i, f: (l, 0, f)),
                   pl.BlockSpec((None, tf, D_MODEL), lambda i, f: (l, f, 0))]
    else:
        w_args = list(w_bf16)
        w_specs = [pl.BlockSpec((D_MODEL, tf), lambda i, f: (0, f)),
                   pl.BlockSpec((D_MODEL, tf), lambda i, f: (0, f)),
                   pl.BlockSpec((tf, D_MODEL), lambda i, f: (f, 0))]
    args = [h2, x1] + w_args + [p["conv_ffn_wb"]]
    specs = [pl.BlockSpec((BM_FFN, D_MODEL), lambda i, f: (i, 0)),
             pl.BlockSpec((BM_FFN, RES_W), lambda i, f: (i, jnp.minimum(f, N_RES - 1)))] + w_specs + [
             pl.BlockSpec((None, CONV_W + 1, tf), lambda i, f: (l, 0, f))]
    out_shape = [jax.ShapeDtypeStruct((rows, D_MODEL), F32)]
    out_specs = [pl.BlockSpec((BM_FFN, D_MODEL), lambda i, f: (i, 0))]
    scratch = []
    if sample:
        args += [p["st_ffn"]]
        specs += [pl.BlockSpec((None, 2, 128, tf), lambda i, f: (l, 0, 0, f))]
    if final:
        args += [p["final_norm_g"]]
        specs += [pl.BlockSpec((1, D_MODEL), lambda i, f: (0, 0))]
    if cast_next is not None:
        chunks_per_tile = N_CAST // n_tiles
        steps_per_chunk = -(-n_f // chunks_per_tile)
        chunk = lambda i, f: (i * chunks_per_tile + f // steps_per_chunk, 0)
        args += list(cast_next)
        specs += [pl.BlockSpec((None, CAST_ROWS, IN_COLS), lambda i, f: (l + 1,) + chunk(i, f)),
                  pl.BlockSpec((None, CAST_ROWS, D_MODEL), lambda i, f: (l + 1,) + chunk(i, f))]
    if sample:
        out_shape += [jax.ShapeDtypeStruct((2, 128, D_FF), F32),
                      jax.ShapeDtypeStruct((D_MODEL, D_FF), BF16),
                      jax.ShapeDtypeStruct((D_MODEL, D_FF), BF16),
                      jax.ShapeDtypeStruct((D_FF, D_MODEL), BF16)]
        out_specs += [pl.BlockSpec((2, 128, tf), lambda i, f: (0, 0, f)),
                      pl.BlockSpec((D_MODEL, tf), lambda i, f: (0, f)),
                      pl.BlockSpec((D_MODEL, tf), lambda i, f: (0, f)),
                      pl.BlockSpec((tf, D_MODEL), lambda i, f: (f, 0))]
        tiles_per_seq = 1
    else:
        tiles_per_seq = 2048 // BM_FFN
        out_shape += [jax.ShapeDtypeStruct((n_tiles, 2, D_FF), F32)]
        out_specs += [pl.BlockSpec((None, 2, tf), lambda i, f: (i, 0, f))]
        scratch = [pltpu.VMEM((n_f, 8, tf), F32)]
    if cast_next is not None:
        out_shape += [jax.ShapeDtypeStruct((D_MODEL, IN_COLS), BF16),
                      jax.ShapeDtypeStruct((D_MODEL, D_MODEL), BF16)]
        out_specs += [pl.BlockSpec((CAST_ROWS, IN_COLS), chunk), pl.BlockSpec((CAST_ROWS, D_MODEL), chunk)]
    outs = pl.pallas_call(
        functools.partial(_ffn_kernel, sample=sample, final=final, cast_next=cast_next is not None,
                          tf=tf, n_f=n_f, tiles_per_seq=tiles_per_seq),
        grid=(n_tiles, n_f),
        in_specs=specs,
        out_specs=out_specs,
        out_shape=out_shape,
        scratch_shapes=scratch,
        compiler_params=pltpu.CompilerParams(
            dimension_semantics=("arbitrary", "arbitrary"), vmem_limit_bytes=VMEM_LIMIT_BYTES),
        name="ffn_sample" if sample else "ffn_prompt",
    )(*args)
    if sample:
        return outs[0], outs[1], outs[2:]
    return outs[0], outs[1][tiles_per_seq - 1::tiles_per_seq], outs[2:]


def kernel(x_prompt, x_sample, state_conv_mix, state_conv_ffn, norm_mix_g, w_in, sgu_norm_g, w_s, b_s,
           conv_mix_w, conv_mix_b, out_norm_a_g, out_norm_b_g, w_out, norm_ffn_g, w_gate, w_up,
           conv_ffn_w, conv_ffn_b, w_down, final_norm_g):
    depth = w_in.shape[0]
    n_seq, seq, _ = x_prompt.shape
    n_dec, dec_seq, _ = x_sample.shape
    assert (seq, n_dec, dec_seq) == (2048, 128, 8)

    row3 = lambda a: a.reshape(depth, 1, a.shape[-1])
    tmajor = lambda a: jnp.swapaxes(a, -3, -2)
    p = {
        "norm_mix_g": row3(norm_mix_g), "sgu_norm_g": row3(sgu_norm_g),
        "out_norm_a_g": row3(out_norm_a_g), "out_norm_b_g": row3(out_norm_b_g),
        "norm_ffn_g": row3(norm_ffn_g),
        "conv_mix_wb": jnp.concatenate([conv_mix_w, conv_mix_b[:, None, :]], axis=1),
        "conv_ffn_wb": jnp.concatenate([conv_ffn_w, conv_ffn_b[:, None, :]], axis=1),
        "w_gate": w_gate, "w_up": w_up, "w_down": w_down,
        "w_s": w_s,
        "b_s_full": jnp.broadcast_to(b_s[..., None], (depth, N_HEADS, CHUNK, HEAD_DIM)),
        "w8": w_s[:, :, :dec_seq, :dec_seq].reshape(depth, -1),
        "b8": b_s[:, :, :dec_seq].reshape(depth, -1),
        "st_ffn": tmajor(state_conv_ffn),
        "final_norm_g": final_norm_g.reshape(1, D_MODEL),
    }
    st_mix = tmajor(state_conv_mix)

    xp = x_prompt.reshape(n_seq * seq, D_MODEL)
    xs = tmajor(x_sample)
    flat = lambda a: a.reshape(dec_seq * n_dec, D_MODEL)
    w_in_b, w_out_b = w_in[0].astype(BF16), w_out[0].astype(BF16)
    mix_p, ffn_p, mix_s, ffn_s, v_s = [], [], [], [], []
    for l in range(depth):
        final = l == depth - 1
        x1_s, h2_s, nm_s, v_rows = _mixer(xs, w_in_b, w_out_b, l, p, sample=True, st_in=st_mix)
        x2_s, nf_s, w_bf16 = _ffn(flat(h2_s), flat(x1_s), l, p, sample=True, final=final)
        x1_p, h2_p, nm_p = _mixer(xp, w_in_b, w_out_b, l, p, sample=False)
        xp, nf_p, proj_next = _ffn(h2_p, x1_p, l, p, sample=False, final=final, w_bf16=w_bf16,
                                   cast_next=None if final else (w_in, w_out))
        if not final:
            w_in_b, w_out_b = proj_next
        xs = x2_s.reshape(dec_seq, n_dec, D_MODEL)
        mix_p.append(nm_p)
        ffn_p.append(nf_p)
        mix_s.append(nm_s)
        ffn_s.append(nf_s)
        v_s.append(v_rows)
    y_prompt = xp.reshape(n_seq, seq, D_MODEL)
    y_sample = tmajor(xs)
    return (y_prompt, y_sample, jnp.stack(mix_p), jnp.stack(ffn_p),
            tmajor(jnp.stack(mix_s)), tmajor(jnp.stack(ffn_s)), tmajor(jnp.stack(v_s)))
```

```python
import functools

import jax
import jax.numpy as jnp
from jax import lax
from jax.experimental import pallas as pl
from jax.experimental.pallas import tpu as pltpu

D_MODEL = 2048
A_WIDTH = 1024
B_WIDTH = 1024
HEAD_DIM = 128
N_HEADS = 8
CHUNK = 128
CONV_W = 3
D_FF = 5632
IN_COLS = 2 * A_WIDTH + 3 * B_WIDTH
EPS = 1e-6

VMEM_LIMIT_BYTES = 56 * 1024 * 1024
VMEM_LIMIT_LARGE_BYTES = 60 * 1024 * 1024

BM_MIX_PROMPT = 512
BM_MIX_SAMPLE = 256
MIX_SUBTILES = 2
BM_FFN = 1024
TF_PROMPT = 512
TF_SAMPLE = 512
TN = 512
RES_W = 256
N_RES = D_MODEL // RES_W
FFN_SUBTILES = 2
CAST_ROWS = 64
N_CAST = D_MODEL // CAST_ROWS

F32 = jnp.float32
BF16 = jnp.bfloat16


def _rms(x, g):
    ms = jnp.mean(x * x, axis=-1, keepdims=True)
    return x * lax.rsqrt(ms + EPS) * g


def _gelu(x):
    c = 0.7978845608028654
    return 0.5 * x * (1.0 + jnp.tanh(c * (x + 0.044715 * (x * x * x))))


def _group_norm(y, g):
    outs = []
    for hd in range(N_HEADS):
        blk = y[:, hd * HEAD_DIM:(hd + 1) * HEAD_DIM]
        ms = jnp.mean(blk * blk, axis=-1, keepdims=True)
        outs.append(blk * lax.rsqrt(ms + EPS))
    return jnp.concatenate(outs, axis=-1) * g


def _shift_rows_prompt(z, carry):
    r1 = pltpu.roll(z, 1, 0)
    r2 = pltpu.roll(z, 2, 0)
    c1 = pltpu.roll(carry, 1, 0)
    c2 = pltpu.roll(carry, 2, 0)
    rid = lax.broadcasted_iota(jnp.int32, carry.shape, 0)
    top1 = jnp.where(rid < 1, c1, r1[0:8])
    top2 = jnp.where(rid < 2, c2, r2[0:8])
    p1 = jnp.concatenate([top1, r1[8:]], axis=0)
    p2 = jnp.concatenate([top2, r2[8:]], axis=0)
    return p1, p2


def _shift_rows_sample(z, st, bs):
    p1 = jnp.concatenate([st[1], z[:7 * bs]], axis=0)
    p2 = jnp.concatenate([st[0], st[1], z[:6 * bs]], axis=0)
    return p1, p2


def _conv3(z, p1, p2, cwb):
    return cwb[3:4] + cwb[0:1] * p2 + cwb[1:2] * p1 + cwb[2:3] * z


def _row_spec(l, n):
    return pl.BlockSpec((None, 1, n), lambda i, l=l: (l, 0, 0))


def _mixer_kernel(*refs, sample, bm, tiles_per_seq):
    it = iter(refs)
    x_ref, g1_ref, w_in_ref, gv_ref = next(it), next(it), next(it), next(it)
    if sample:
        w8_ref, b8_ref, st_in_ref = next(it), next(it), next(it)
    else:
        ws_ref, bs_ref = next(it), next(it)
    cwb_ref, ga_ref, gb_ref, w_out_ref, g2_ref = (next(it) for _ in range(5))
    x1_ref, h2_ref, st_out_ref = next(it), next(it), next(it)
    if sample:
        v_out_ref = next(it)
    else:
        carry_ref = next(it)

    def in_proj(x):
        h = _rms(x, g1_ref[...]).astype(BF16)
        return lambda c0: jnp.dot(h, w_in_ref[:, c0:c0 + 1024], preferred_element_type=F32)

    def out_proj(x, y_a, y_b):
        y = jnp.concatenate([y_a, y_b], axis=1)
        x1 = x + jnp.dot(y, w_out_ref[...], preferred_element_type=F32)
        return x1, _rms(x1, g2_ref[...]).astype(BF16)

    if sample:
        bs = bm // 8
        x = x_ref[...].reshape(bm, D_MODEL)
        proj = in_proj(x)
        v = _rms(_gelu(proj(A_WIDTH)), gv_ref[...])
        u = _gelu(proj(0))
        v_out_ref[...] = v.reshape(8, bs, A_WIDTH)
        heads = []
        for hd in range(N_HEADS):
            vh = [v[t * bs:(t + 1) * bs, hd * HEAD_DIM:(hd + 1) * HEAD_DIM] for t in range(8)]
            rows = []
            for i in range(8):
                acc = w8_ref[(hd * 8 + i) * 8] * vh[0]
                for j in range(1, i + 1):
                    acc = acc + w8_ref[(hd * 8 + i) * 8 + j] * vh[j]
                rows.append(acc + b8_ref[hd * 8 + i])
            heads.append(jnp.concatenate(rows, axis=0))
        mixed = jnp.concatenate(heads, axis=1)
        y_a = _group_norm(u * mixed, ga_ref[...]).astype(BF16)
        z = proj(2 * A_WIDTH + B_WIDTH) * proj(2 * A_WIDTH + 2 * B_WIDTH)
        p1, p2 = _shift_rows_sample(z, st_in_ref[...], bs)
        st_out_ref[...] = z[6 * bs:].reshape(2, bs, B_WIDTH)
        zc = _conv3(z, p1, p2, cwb_ref[...])
        y_b = _group_norm(proj(2 * A_WIDTH) * zc, gb_ref[...]).astype(BF16)
        x1, h2 = out_proj(x, y_a, y_b)
        x1_ref[...] = x1.reshape(x1_ref.shape)
        h2_ref[...] = h2.reshape(h2_ref.shape)
        return

    m = bm // MIX_SUBTILES
    nch = m // CHUNK
    row = lax.broadcasted_iota(jnp.int32, (CHUNK, CHUNK), 0)
    col = lax.broadcasted_iota(jnp.int32, (CHUNK, CHUNK), 1)
    wm = [jnp.where(row >= col, ws_ref[hd], 0.0).astype(BF16) for hd in range(N_HEADS)]
    first = (pl.program_id(0) % tiles_per_seq) == 0
    carry = jnp.where(first, 0.0, carry_ref[...])
    for s in range(MIX_SUBTILES):
        x = x_ref[s * m:(s + 1) * m, :]
        proj = in_proj(x)
        v = _rms(_gelu(proj(A_WIDTH)), gv_ref[...])
        u = _gelu(proj(0))
        vb = v.astype(BF16)
        heads = []
        for hd in range(N_HEADS):
            vh = jnp.concatenate(
                [vb[c * CHUNK:(c + 1) * CHUNK, hd * HEAD_DIM:(hd + 1) * HEAD_DIM] for c in range(nch)], axis=1)
            mh = jnp.dot(wm[hd], vh, preferred_element_type=F32)
            bias = bs_ref[hd]
            heads.append(jnp.concatenate(
                [mh[:, c * HEAD_DIM:(c + 1) * HEAD_DIM] + bias for c in range(nch)], axis=0))
        mixed = jnp.concatenate(heads, axis=1)
        y_a = _group_norm(u * mixed, ga_ref[...]).astype(BF16)
        z = proj(2 * A_WIDTH + B_WIDTH) * proj(2 * A_WIDTH + 2 * B_WIDTH)
        p1, p2 = _shift_rows_prompt(z, carry)
        carry = z[m - 8:m]
        zc = _conv3(z, p1, p2, cwb_ref[...])
        y_b = _group_norm(proj(2 * A_WIDTH) * zc, gb_ref[...]).astype(BF16)
        x1, h2 = out_proj(x, y_a, y_b)
        x1_ref[s * m:(s + 1) * m, :] = x1
        h2_ref[s * m:(s + 1) * m, :] = h2
    carry_ref[...] = carry
    st_out_ref[...] = carry[6:8]


def _mixer(x, w_in_b, w_out_b, l, p, *, sample, st_in=None):
    rows = 8 * 128 if sample else x.shape[0]
    bm = BM_MIX_SAMPLE if sample else BM_MIX_PROMPT
    bs = bm // 8
    n_tiles = rows // bm
    if sample:
        x_spec = pl.BlockSpec((8, bs, D_MODEL), lambda i: (0, i, 0))
    else:
        x_spec = pl.BlockSpec((bm, D_MODEL), lambda i: (i, 0))
    args = [x, p["norm_mix_g"], w_in_b, p["sgu_norm_g"]]
    specs = [x_spec, _row_spec(l, D_MODEL),
             pl.BlockSpec((D_MODEL, IN_COLS), lambda i: (0, 0), pipeline_mode=pl.Buffered(1)),
             _row_spec(l, A_WIDTH)]
    if sample:
        args += [p["w8"][l], p["b8"][l], st_in]
        specs += [pl.BlockSpec(memory_space=pltpu.SMEM), pl.BlockSpec(memory_space=pltpu.SMEM),
                  pl.BlockSpec((None, 2, bs, B_WIDTH), lambda i: (l, 0, i, 0))]
    else:
        args += [p["w_s"], p["b_s_full"]]
        specs += [pl.BlockSpec((None, N_HEADS, CHUNK, CHUNK), lambda i: (l, 0, 0, 0)),
                  pl.BlockSpec((None, N_HEADS, CHUNK, HEAD_DIM), lambda i: (l, 0, 0, 0))]
    args += [p["conv_mix_wb"], p["out_norm_a_g"], p["out_norm_b_g"], w_out_b, p["norm_ffn_g"]]
    specs += [pl.BlockSpec((None, CONV_W + 1, B_WIDTH), lambda i: (l, 0, 0)),
              _row_spec(l, A_WIDTH), _row_spec(l, B_WIDTH),
              pl.BlockSpec((D_MODEL, D_MODEL), lambda i: (0, 0), pipeline_mode=pl.Buffered(1)),
              _row_spec(l, D_MODEL)]

    out_shape = [jax.ShapeDtypeStruct(x.shape, F32), jax.ShapeDtypeStruct(x.shape, BF16)]
    out_specs = [x_spec, x_spec]
    scratch = []
    if sample:
        out_shape += [jax.ShapeDtypeStruct((2, 128, B_WIDTH), F32),
                      jax.ShapeDtypeStruct((8, 128, A_WIDTH), F32)]
        out_specs += [pl.BlockSpec((2, bs, B_WIDTH), lambda i: (0, i, 0)),
                      pl.BlockSpec((8, bs, A_WIDTH), lambda i: (0, i, 0))]
        tiles_per_seq = 1
    else:
        tiles_per_seq = 2048 // bm
        out_shape += [jax.ShapeDtypeStruct((rows // 2048, 2, B_WIDTH), F32)]
        out_specs += [pl.BlockSpec((None, 2, B_WIDTH), lambda i: (i // tiles_per_seq, 0, 0))]
        scratch = [pltpu.VMEM((8, B_WIDTH), F32)]

    return pl.pallas_call(
        functools.partial(_mixer_kernel, sample=sample, bm=bm, tiles_per_seq=tiles_per_seq),
        grid=(n_tiles,),
        in_specs=specs,
        out_specs=out_specs,
        out_shape=out_shape,
        scratch_shapes=scratch,
        compiler_params=pltpu.CompilerParams(
            dimension_semantics=("arbitrary",),
            vmem_limit_bytes=VMEM_LIMIT_BYTES if sample else VMEM_LIMIT_LARGE_BYTES),
        name="mixer_sample" if sample else "mixer_prompt",
    )(*args)


def _ffn_kernel(*refs, sample, final, cast_next, tf, n_f, tiles_per_seq):
    it = iter(refs)
    h_ref, xr_ref, wg_ref, wu_ref, wd_ref, cwb_ref = (next(it) for _ in range(6))
    st_in_ref = next(it) if sample else None
    gf_ref = next(it) if final else None
    if cast_next:
        win_ref, wout_ref = next(it), next(it)
    o_ref, st_out_ref = next(it), next(it)
    if sample:
        wg_out_ref, wu_out_ref, wd_out_ref = next(it), next(it), next(it)
    if cast_next:
        win_out_ref, wout_out_ref = next(it), next(it)
    if not sample:
        carry_ref = next(it)

    i = pl.program_id(0)
    f = pl.program_id(1)

    @pl.when(f == 0)
    def _():
        o_ref[...] = jnp.zeros_like(o_ref)

    @pl.when(f < N_RES)
    def _():
        r0 = pl.multiple_of(f * RES_W, RES_W)
        o_ref[:, pl.ds(r0, RES_W)] += xr_ref[...]

    cwb = cwb_ref[...]
    if cast_next:
        win_out_ref[...] = win_ref[...].astype(BF16)
        wout_out_ref[...] = wout_ref[...].astype(BF16)

    def act_down(r0, m, g, u, p1, p2, wd):
        gc = _conv3(g, p1, p2, cwb)
        a = (gc / (1.0 + jnp.exp(-gc)) * u).astype(BF16)
        for n in range(D_MODEL // TN):
            o_ref[r0:r0 + m, n * TN:(n + 1) * TN] += jnp.dot(
                a, wd[:, n * TN:(n + 1) * TN], preferred_element_type=F32)

    if sample:
        wg = wg_ref[...].astype(BF16)
        wu = wu_ref[...].astype(BF16)
        wd = wd_ref[...].astype(BF16)
        wg_out_ref[...] = wg
        wu_out_ref[...] = wu
        wd_out_ref[...] = wd
        h = h_ref[...]
        g = jnp.dot(h, wg, preferred_element_type=F32)
        u = jnp.dot(h, wu, preferred_element_type=F32)
        p1, p2 = _shift_rows_sample(g, st_in_ref[...], 128)
        st_out_ref[...] = g[6 * 128:].reshape(2, 128, tf)
        act_down(0, BM_FFN, g, u, p1, p2, wd)
    else:
        wg, wu, wd = wg_ref[...], wu_ref[...], wd_ref[...]
        first = (i % tiles_per_seq) == 0
        carry = jnp.where(first, 0.0, carry_ref[f])
        m = BM_FFN // FFN_SUBTILES
        for s in range(FFN_SUBTILES):
            h = h_ref[s * m:(s + 1) * m, :]
            g = jnp.dot(h, wg, preferred_element_type=F32)
            u = jnp.dot(h, wu, preferred_element_type=F32)
            p1, p2 = _shift_rows_prompt(g, carry)
            carry = g[m - 8:m]
            act_down(s * m, m, g, u, p1, p2, wd)
        carry_ref[f] = carry
        st_out_ref[...] = carry[6:8]

    if final:
        @pl.when(f == n_f - 1)
        def _():
            o_ref[...] = _rms(o_ref[...], gf_ref[...])


def _ffn(h2, x1, l, p, *, sample, final, w_bf16=None, cast_next=None):
    rows = h2.shape[0]
    n_tiles = rows // BM_FFN
    tf = TF_SAMPLE if sample else TF_PROMPT
    n_f = D_FF // tf
    assert n_f >= N_RES
    if sample:
        w_args = [p["w_gate"], p["w_up"], p["w_down"]]
        w_specs = [pl.BlockSpec((None, D_MODEL, tf), lambda i, f: (l, 0, f)),
                   pl.BlockSpec((None, D_MODEL, tf), lambda i, f: (l, 0, f)),
                   pl.BlockSpec((None, tf, D_MODEL), lambda i, f: (l, f, 0))]
    else:
        w_args = list(w_bf16)
        w_specs = [pl.BlockSpec((D_MODEL, tf), lambda i, f: (0, f)),
                   pl.BlockSpec((D_MODEL, tf), lambda i, f: (0, f)),
                   pl.BlockSpec((tf, D_MODEL), lambda i, f: (f, 0))]
    args = [h2, x1] + w_args + [p["conv_ffn_wb"]]
    specs = [pl.BlockSpec((BM_FFN, D_MODEL), lambda i, f: (i, 0)),
             pl.BlockSpec((BM_FFN, RES_W), lambda i, f: (i, jnp.minimum(f, N_RES - 1)))] + w_specs + [
             pl.BlockSpec((None, CONV_W + 1, tf), lambda i, f: (l, 0, f))]
    out_shape = [jax.ShapeDtypeStruct((rows, D_MODEL), F32)]
    out_specs = [pl.BlockSpec((BM_FFN, D_MODEL), lambda i, f: (i, 0))]
    scratch = []
    if sample:
        args += [p["st_ffn"]]
        specs += [pl.BlockSpec((None, 2, 128, tf), lambda i, f: (l, 0, 0, f))]
    if final:
        args += [p["final_norm_g"]]
        specs += [pl.BlockSpec((1, D_MODEL), lambda i, f: (0, 0))]
    if cast_next is not None:
        chunks_per_tile = N_CAST // n_tiles
        steps_per_chunk = -(-n_f // chunks_per_tile)
        chunk = lambda i, f: (i * chunks_per_tile + f // steps_per_chunk, 0)
        args += list(cast_next)
        specs += [pl.BlockSpec((None, CAST_ROWS, IN_COLS), lambda i, f: (l + 1,) + chunk(i, f)),
                  pl.BlockSpec((None, CAST_ROWS, D_MODEL), lambda i, f: (l + 1,) + chunk(i, f))]
    if sample:
        out_shape += [jax.ShapeDtypeStruct((2, 128, D_FF), F32),
                      jax.ShapeDtypeStruct((D_MODEL, D_FF), BF16),
                      jax.ShapeDtypeStruct((D_MODEL, D_FF), BF16),
                      jax.ShapeDtypeStruct((D_FF, D_MODEL), BF16)]
        out_specs += [pl.BlockSpec((2, 128, tf), lambda i, f: (0, 0, f)),
                      pl.BlockSpec((D_MODEL, tf), lambda i, f: (0, f)),
                      pl.BlockSpec((D_MODEL, tf), lambda i, f: (0, f)),
                      pl.BlockSpec((tf, D_MODEL), lambda i, f: (f, 0))]
        tiles_per_seq = 1
    else:
        tiles_per_seq = 2048 // BM_FFN
        out_shape += [jax.ShapeDtypeStruct((n_tiles, 2, D_FF), F32)]
        out_specs += [pl.BlockSpec((None, 2, tf), lambda i, f: (i, 0, f))]
        scratch = [pltpu.VMEM((n_f, 8, tf), F32)]
    if cast_next is not None:
        out_shape += [jax.ShapeDtypeStruct((D_MODEL, IN_COLS), BF16),
                      jax.ShapeDtypeStruct((D_MODEL, D_MODEL), BF16)]
        out_specs += [pl.BlockSpec((CAST_ROWS, IN_COLS), chunk), pl.BlockSpec((CAST_ROWS, D_MODEL), chunk)]
    outs = pl.pallas_call(
        functools.partial(_ffn_kernel, sample=sample, final=final, cast_next=cast_next is not None,
                          tf=tf, n_f=n_f, tiles_per_seq=tiles_per_seq),
        grid=(n_tiles, n_f),
        in_specs=specs,
        out_specs=out_specs,
        out_shape=out_shape,
        scratch_shapes=scratch,
        compiler_params=pltpu.CompilerParams(
            dimension_semantics=("arbitrary", "arbitrary"),
            vmem_limit_bytes=VMEM_LIMIT_LARGE_BYTES if sample else VMEM_LIMIT_BYTES),
        name="ffn_sample" if sample else "ffn_prompt",
    )(*args)
    if sample:
        return outs[0], outs[1], outs[2:]
    return outs[0], outs[1][tiles_per_seq - 1::tiles_per_seq], outs[2:]


def kernel(x_prompt, x_sample, state_conv_mix, state_conv_ffn, norm_mix_g, w_in, sgu_norm_g, w_s, b_s,
           conv_mix_w, conv_mix_b, out_norm_a_g, out_norm_b_g, w_out, norm_ffn_g, w_gate, w_up,
           conv_ffn_w, conv_ffn_b, w_down, final_norm_g):
    depth = w_in.shape[0]
    n_seq, seq, _ = x_prompt.shape
    n_dec, dec_seq, _ = x_sample.shape
    assert (seq, n_dec, dec_seq) == (2048, 128, 8)

    row3 = lambda a: a.reshape(depth, 1, a.shape[-1])
    tmajor = lambda a: jnp.swapaxes(a, -3, -2)
    p = {
        "norm_mix_g": row3(norm_mix_g), "sgu_norm_g": row3(sgu_norm_g),
        "out_norm_a_g": row3(out_norm_a_g), "out_norm_b_g": row3(out_norm_b_g),
        "norm_ffn_g": row3(norm_ffn_g),
        "conv_mix_wb": jnp.concatenate([conv_mix_w, conv_mix_b[:, None, :]], axis=1),
        "conv_ffn_wb": jnp.concatenate([conv_ffn_w, conv_ffn_b[:, None, :]], axis=1),
        "w_gate": w_gate, "w_up": w_up, "w_down": w_down,
        "w_s": w_s,
        "b_s_full": jnp.broadcast_to(b_s[..., None], (depth, N_HEADS, CHUNK, HEAD_DIM)),
        "w8": w_s[:, :, :dec_seq, :dec_seq].reshape(depth, -1),
        "b8": b_s[:, :, :dec_seq].reshape(depth, -1),
        "st_ffn": tmajor(state_conv_ffn),
        "final_norm_g": final_norm_g.reshape(1, D_MODEL),
    }
    st_mix = tmajor(state_conv_mix)

    xp = x_prompt.reshape(n_seq * seq, D_MODEL)
    xs = tmajor(x_sample)
    flat = lambda a: a.reshape(dec_seq * n_dec, D_MODEL)
    w_in_b, w_out_b = w_in[0].astype(BF16), w_out[0].astype(BF16)
    mix_p, ffn_p, mix_s, ffn_s, v_s = [], [], [], [], []
    for l in range(depth):
        final = l == depth - 1
        x1_s, h2_s, nm_s, v_rows = _mixer(xs, w_in_b, w_out_b, l, p, sample=True, st_in=st_mix)
        x2_s, nf_s, w_bf16 = _ffn(flat(h2_s), flat(x1_s), l, p, sample=True, final=final)
        x1_p, h2_p, nm_p = _mixer(xp, w_in_b, w_out_b, l, p, sample=False)
        xp, nf_p, proj_next = _ffn(h2_p, x1_p, l, p, sample=False, final=final, w_bf16=w_bf16,
                                   cast_next=None if final else (w_in, w_out))
        if not final:
            w_in_b, w_out_b = proj_next
        xs = x2_s.reshape(dec_seq, n_dec, D_MODEL)
        mix_p.append(nm_p)
        ffn_p.append(nf_p)
        mix_s.append(nm_s)
        ffn_s.append(nf_s)
        v_s.append(v_rows)
    y_prompt = xp.reshape(n_seq, seq, D_MODEL)
    y_sample = tmajor(xs)
    return (y_prompt, y_sample, jnp.stack(mix_p), jnp.stack(ffn_p),
            tmajor(jnp.stack(mix_s)), tmajor(jnp.stack(ffn_s)), tmajor(jnp.stack(v_s)))
```

```python
import functools

import jax
import jax.numpy as jnp
from jax import lax
from jax.experimental import pallas as pl
from jax.experimental.pallas import tpu as pltpu

D_MODEL = 2048
A_WIDTH = 1024
B_WIDTH = 1024
HEAD_DIM = 128
N_HEADS = 8
CHUNK = 128
CONV_W = 3
D_FF = 5632
IN_COLS = 2 * A_WIDTH + 3 * B_WIDTH
EPS = 1e-6

VMEM_LIMIT_BYTES = 56 * 1024 * 1024
VMEM_LIMIT_LARGE_BYTES = 60 * 1024 * 1024

BM_MIX_PROMPT = 512
BM_MIX_SAMPLE = 256
BM_FFN = 1024
TF_PROMPT = 512
TF_SAMPLE = 512
TN = 512
RES_W = 256
N_RES = D_MODEL // RES_W
CAST_ROWS = 64
N_CAST = D_MODEL // CAST_ROWS

F32 = jnp.float32
BF16 = jnp.bfloat16


def _rms(x, g):
    ms = jnp.mean(x * x, axis=-1, keepdims=True)
    return x * lax.rsqrt(ms + EPS) * g


def _gelu(x):
    c = 0.7978845608028654
    return 0.5 * x * (1.0 + jnp.tanh(c * (x + 0.044715 * (x * x * x))))


def _group_norm(y, g):
    outs = []
    for hd in range(N_HEADS):
        blk = y[:, hd * HEAD_DIM:(hd + 1) * HEAD_DIM]
        ms = jnp.mean(blk * blk, axis=-1, keepdims=True)
        outs.append(blk * lax.rsqrt(ms + EPS))
    return jnp.concatenate(outs, axis=-1) * g


def _shift_rows_prompt(z, carry):
    r1 = pltpu.roll(z, 1, 0)
    r2 = pltpu.roll(z, 2, 0)
    c1 = pltpu.roll(carry, 1, 0)
    c2 = pltpu.roll(carry, 2, 0)
    rid = lax.broadcasted_iota(jnp.int32, carry.shape, 0)
    top1 = jnp.where(rid < 1, c1, r1[0:8])
    top2 = jnp.where(rid < 2, c2, r2[0:8])
    p1 = jnp.concatenate([top1, r1[8:]], axis=0)
    p2 = jnp.concatenate([top2, r2[8:]], axis=0)
    return p1, p2


def _shift_rows_sample(z, st, bs):
    p1 = jnp.concatenate([st[1], z[:7 * bs]], axis=0)
    p2 = jnp.concatenate([st[0], st[1], z[:6 * bs]], axis=0)
    return p1, p2


def _conv3(z, p1, p2, cwb):
    return cwb[3:4] + cwb[0:1] * p2 + cwb[1:2] * p1 + cwb[2:3] * z


def _row_spec(l, n):
    return pl.BlockSpec((None, 1, n), lambda i, l=l: (l, 0, 0))


def _mixer_kernel(*refs, sample, bm, tiles_per_seq):
    it = iter(refs)
    x_ref, g1_ref, w_in_ref, gv_ref = next(it), next(it), next(it), next(it)
    if sample:
        w8_ref, b8_ref, st_in_ref = next(it), next(it), next(it)
    else:
        ws_ref, bs_ref = next(it), next(it)
    cwb_ref, ga_ref, gb_ref, w_out_ref, g2_ref = (next(it) for _ in range(5))
    x1_ref, h2_ref, st_out_ref = next(it), next(it), next(it)
    if sample:
        v_out_ref = next(it)
    else:
        carry_ref = next(it)

    def in_proj(x):
        h = _rms(x, g1_ref[...]).astype(BF16)
        return lambda c0: jnp.dot(h, w_in_ref[:, c0:c0 + 1024], preferred_element_type=F32)

    def out_proj(x, y_a, y_b):
        y = jnp.concatenate([y_a, y_b], axis=1)
        x1 = x + jnp.dot(y, w_out_ref[...], preferred_element_type=F32)
        return x1, _rms(x1, g2_ref[...]).astype(BF16)

    if sample:
        bs = bm // 8
        x = x_ref[...].reshape(bm, D_MODEL)
        proj = in_proj(x)
        v = _rms(_gelu(proj(A_WIDTH)), gv_ref[...])
        u = _gelu(proj(0))
        v_out_ref[...] = v.reshape(8, bs, A_WIDTH)
        heads = []
        for hd in range(N_HEADS):
            vh = [v[t * bs:(t + 1) * bs, hd * HEAD_DIM:(hd + 1) * HEAD_DIM] for t in range(8)]
            rows = []
            for i in range(8):
                acc = w8_ref[(hd * 8 + i) * 8] * vh[0]
                for j in range(1, i + 1):
                    acc = acc + w8_ref[(hd * 8 + i) * 8 + j] * vh[j]
                rows.append(acc + b8_ref[hd * 8 + i])
            heads.append(jnp.concatenate(rows, axis=0))
        mixed = jnp.concatenate(heads, axis=1)
        y_a = _group_norm(u * mixed, ga_ref[...]).astype(BF16)
        z = proj(2 * A_WIDTH + B_WIDTH) * proj(2 * A_WIDTH + 2 * B_WIDTH)
        p1, p2 = _shift_rows_sample(z, st_in_ref[...], bs)
        st_out_ref[...] = z[6 * bs:].reshape(2, bs, B_WIDTH)
        zc = _conv3(z, p1, p2, cwb_ref[...])
        y_b = _group_norm(proj(2 * A_WIDTH) * zc, gb_ref[...]).astype(BF16)
        x1, h2 = out_proj(x, y_a, y_b)
        x1_ref[...] = x1.reshape(x1_ref.shape)
        h2_ref[...] = h2.reshape(h2_ref.shape)
        return

    nch = bm // CHUNK
    x = x_ref[...]
    proj = in_proj(x)
    v = _rms(_gelu(proj(A_WIDTH)), gv_ref[...])
    u = _gelu(proj(0))
    vb = v.astype(BF16)
    row = lax.broadcasted_iota(jnp.int32, (CHUNK, CHUNK), 0)
    col = lax.broadcasted_iota(jnp.int32, (CHUNK, CHUNK), 1)
    heads = []
    for hd in range(N_HEADS):
        wm = jnp.where(row >= col, ws_ref[hd], 0.0).astype(BF16)
        vh = jnp.concatenate(
            [vb[c * CHUNK:(c + 1) * CHUNK, hd * HEAD_DIM:(hd + 1) * HEAD_DIM] for c in range(nch)], axis=1)
        mh = jnp.dot(wm, vh, preferred_element_type=F32)
        bias = bs_ref[hd]
        heads.append(jnp.concatenate(
            [mh[:, c * HEAD_DIM:(c + 1) * HEAD_DIM] + bias for c in range(nch)], axis=0))
    mixed = jnp.concatenate(heads, axis=1)
    y_a = _group_norm(u * mixed, ga_ref[...]).astype(BF16)
    z = proj(2 * A_WIDTH + B_WIDTH) * proj(2 * A_WIDTH + 2 * B_WIDTH)
    first = (pl.program_id(0) % tiles_per_seq) == 0
    carry = jnp.where(first, 0.0, carry_ref[...])
    p1, p2 = _shift_rows_prompt(z, carry)
    carry_ref[...] = z[bm - 8:bm]
    st_out_ref[...] = z[bm - 2:bm]
    zc = _conv3(z, p1, p2, cwb_ref[...])
    y_b = _group_norm(proj(2 * A_WIDTH) * zc, gb_ref[...]).astype(BF16)
    x1, h2 = out_proj(x, y_a, y_b)
    x1_ref[...] = x1
    h2_ref[...] = h2


def _mixer(x, w_in_b, w_out_b, l, p, *, sample, st_in=None):
    rows = 8 * 128 if sample else x.shape[0]
    bm = BM_MIX_SAMPLE if sample else BM_MIX_PROMPT
    bs = bm // 8
    n_tiles = rows // bm
    if sample:
        x_spec = pl.BlockSpec((8, bs, D_MODEL), lambda i: (0, i, 0))
    else:
        x_spec = pl.BlockSpec((bm, D_MODEL), lambda i: (i, 0))
    args = [x, p["norm_mix_g"], w_in_b, p["sgu_norm_g"]]
    specs = [x_spec, _row_spec(l, D_MODEL),
             pl.BlockSpec((D_MODEL, IN_COLS), lambda i: (0, 0), pipeline_mode=pl.Buffered(1)),
             _row_spec(l, A_WIDTH)]
    if sample:
        args += [p["w8"][l], p["b8"][l], st_in]
        specs += [pl.BlockSpec(memory_space=pltpu.SMEM), pl.BlockSpec(memory_space=pltpu.SMEM),
                  pl.BlockSpec((None, 2, bs, B_WIDTH), lambda i: (l, 0, i, 0))]
    else:
        args += [p["w_s"], p["b_s_full"]]
        specs += [pl.BlockSpec((None, N_HEADS, CHUNK, CHUNK), lambda i: (l, 0, 0, 0)),
                  pl.BlockSpec((None, N_HEADS, CHUNK, HEAD_DIM), lambda i: (l, 0, 0, 0))]
    args += [p["conv_mix_wb"], p["out_norm_a_g"], p["out_norm_b_g"], w_out_b, p["norm_ffn_g"]]
    specs += [pl.BlockSpec((None, CONV_W + 1, B_WIDTH), lambda i: (l, 0, 0)),
              _row_spec(l, A_WIDTH), _row_spec(l, B_WIDTH),
              pl.BlockSpec((D_MODEL, D_MODEL), lambda i: (0, 0), pipeline_mode=pl.Buffered(1)),
              _row_spec(l, D_MODEL)]

    out_shape = [jax.ShapeDtypeStruct(x.shape, F32), jax.ShapeDtypeStruct(x.shape, BF16)]
    out_specs = [x_spec, x_spec]
    scratch = []
    if sample:
        out_shape += [jax.ShapeDtypeStruct((2, 128, B_WIDTH), F32),
                      jax.ShapeDtypeStruct((8, 128, A_WIDTH), F32)]
        out_specs += [pl.BlockSpec((2, bs, B_WIDTH), lambda i: (0, i, 0)),
                      pl.BlockSpec((8, bs, A_WIDTH), lambda i: (0, i, 0))]
        tiles_per_seq = 1
    else:
        tiles_per_seq = 2048 // bm
        out_shape += [jax.ShapeDtypeStruct((rows // 2048, 2, B_WIDTH), F32)]
        out_specs += [pl.BlockSpec((None, 2, B_WIDTH), lambda i: (i // tiles_per_seq, 0, 0))]
        scratch = [pltpu.VMEM((8, B_WIDTH), F32)]

    return pl.pallas_call(
        functools.partial(_mixer_kernel, sample=sample, bm=bm, tiles_per_seq=tiles_per_seq),
        grid=(n_tiles,),
        in_specs=specs,
        out_specs=out_specs,
        out_shape=out_shape,
        scratch_shapes=scratch,
        compiler_params=pltpu.CompilerParams(
            dimension_semantics=("arbitrary",),
            vmem_limit_bytes=VMEM_LIMIT_BYTES if sample else VMEM_LIMIT_LARGE_BYTES),
        name="mixer_sample" if sample else "mixer_prompt",
    )(*args)


def _ffn_kernel(*refs, sample, final, cast_next, tf, n_f, tiles_per_seq):
    it = iter(refs)
    h_ref, xr_ref, wg_ref, wu_ref, wd_ref, cwb_ref = (next(it) for _ in range(6))
    st_in_ref = next(it) if sample else None
    gf_ref = next(it) if final else None
    if cast_next:
        win_ref, wout_ref = next(it), next(it)
    o_ref, st_out_ref = next(it), next(it)
    if sample:
        wg_out_ref, wu_out_ref, wd_out_ref = next(it), next(it), next(it)
    if cast_next:
        win_out_ref, wout_out_ref = next(it), next(it)
    if not sample:
        carry_ref = next(it)

    i = pl.program_id(0)
    f = pl.program_id(1)

    @pl.when(f == 0)
    def _():
        o_ref[...] = jnp.zeros_like(o_ref)

    @pl.when(f < N_RES)
    def _():
        r0 = pl.multiple_of(f * RES_W, RES_W)
        o_ref[:, pl.ds(r0, RES_W)] += xr_ref[...]

    if cast_next:
        win_out_ref[...] = win_ref[...].astype(BF16)
        wout_out_ref[...] = wout_ref[...].astype(BF16)

    if sample:
        wg = wg_ref[...].astype(BF16)
        wu = wu_ref[...].astype(BF16)
        wd = wd_ref[...].astype(BF16)
        wg_out_ref[...] = wg
        wu_out_ref[...] = wu
        wd_out_ref[...] = wd
    else:
        wg, wu, wd = wg_ref[...], wu_ref[...], wd_ref[...]
    h = h_ref[...]
    g = jnp.dot(h, wg, preferred_element_type=F32)
    u = jnp.dot(h, wu, preferred_element_type=F32)
    if sample:
        p1, p2 = _shift_rows_sample(g, st_in_ref[...], 128)
        st_out_ref[...] = g[6 * 128:].reshape(2, 128, tf)
    else:
        first = (i % tiles_per_seq) == 0
        carry = jnp.where(first, 0.0, carry_ref[f])
        p1, p2 = _shift_rows_prompt(g, carry)
        carry_ref[f] = g[BM_FFN - 8:BM_FFN]
        st_out_ref[...] = g[BM_FFN - 2:BM_FFN]
    gc = _conv3(g, p1, p2, cwb_ref[...])
    a = (gc / (1.0 + jnp.exp(-gc)) * u).astype(BF16)
    for n in range(D_MODEL // TN):
        o_ref[:, n * TN:(n + 1) * TN] += jnp.dot(
            a, wd[:, n * TN:(n + 1) * TN], preferred_element_type=F32)

    if final:
        @pl.when(f == n_f - 1)
        def _():
            o_ref[...] = _rms(o_ref[...], gf_ref[...])


def _ffn(h2, x1, l, p, *, sample, final, w_bf16=None, cast_next=None):
    rows = h2.shape[0]
    n_tiles = rows // BM_FFN
    tf = TF_SAMPLE if sample else TF_PROMPT
    n_f = D_FF // tf
    assert n_f >= N_RES
    if sample:
        w_args = [p["w_gate"], p["w_up"], p["w_down"]]
        w_specs = [pl.BlockSpec((None, D_MODEL, tf), lambda i, f: (l, 0, f)),
                   pl.BlockSpec((None, D_MODEL, tf), lambda i, f: (l, 0, f)),
                   pl.BlockSpec((None, tf, D_MODEL), lambda i, f: (l, f, 0))]
    else:
        w_args = list(w_bf16)
        w_specs = [pl.BlockSpec((D_MODEL, tf), lambda i, f: (0, f)),
                   pl.BlockSpec((D_MODEL, tf), lambda i, f: (0, f)),
                   pl.BlockSpec((tf, D_MODEL), lambda i, f: (f, 0))]
    args = [h2, x1] + w_args + [p["conv_ffn_wb"]]
    specs = [pl.BlockSpec((BM_FFN, D_MODEL), lambda i, f: (i, 0)),
             pl.BlockSpec((BM_FFN, RES_W), lambda i, f: (i, jnp.minimum(f, N_RES - 1)))] + w_specs + [
             pl.BlockSpec((None, CONV_W + 1, tf), lambda i, f: (l, 0, f))]
    out_shape = [jax.ShapeDtypeStruct((rows, D_MODEL), F32)]
    out_specs = [pl.BlockSpec((BM_FFN, D_MODEL), lambda i, f: (i, 0))]
    scratch = []
    if sample:
        args += [p["st_ffn"]]
        specs += [pl.BlockSpec((None, 2, 128, tf), lambda i, f: (l, 0, 0, f))]
    if final:
        args += [p["final_norm_g"]]
        specs += [pl.BlockSpec((1, D_MODEL), lambda i, f: (0, 0))]
    if cast_next is not None:
        chunks_per_tile = N_CAST // n_tiles
        steps_per_chunk = -(-n_f // chunks_per_tile)
        chunk = lambda i, f: (i * chunks_per_tile + f // steps_per_chunk, 0)
        args += list(cast_next)
        specs += [pl.BlockSpec((None, CAST_ROWS, IN_COLS), lambda i, f: (l + 1,) + chunk(i, f)),
                  pl.BlockSpec((None, CAST_ROWS, D_MODEL), lambda i, f: (l + 1,) + chunk(i, f))]
    if sample:
        out_shape += [jax.ShapeDtypeStruct((2, 128, D_FF), F32),
                      jax.ShapeDtypeStruct((D_MODEL, D_FF), BF16),
                      jax.ShapeDtypeStruct((D_MODEL, D_FF), BF16),
                      jax.ShapeDtypeStruct((D_FF, D_MODEL), BF16)]
        out_specs += [pl.BlockSpec((2, 128, tf), lambda i, f: (0, 0, f)),
                      pl.BlockSpec((D_MODEL, tf), lambda i, f: (0, f)),
                      pl.BlockSpec((D_MODEL, tf), lambda i, f: (0, f)),
                      pl.BlockSpec((tf, D_MODEL), lambda i, f: (f, 0))]
        tiles_per_seq = 1
    else:
        tiles_per_seq = 2048 // BM_FFN
        out_shape += [jax.ShapeDtypeStruct((n_tiles, 2, D_FF), F32)]
        out_specs += [pl.BlockSpec((None, 2, tf), lambda i, f: (i, 0, f))]
        scratch = [pltpu.VMEM((n_f, 8, tf), F32)]
    if cast_next is not None:
        out_shape += [jax.ShapeDtypeStruct((D_MODEL, IN_COLS), BF16),
                      jax.ShapeDtypeStruct((D_MODEL, D_MODEL), BF16)]
        out_specs += [pl.BlockSpec((CAST_ROWS, IN_COLS), chunk), pl.BlockSpec((CAST_ROWS, D_MODEL), chunk)]
    outs = pl.pallas_call(
        functools.partial(_ffn_kernel, sample=sample, final=final, cast_next=cast_next is not None,
                          tf=tf, n_f=n_f, tiles_per_seq=tiles_per_seq),
        grid=(n_tiles, n_f),
        in_specs=specs,
        out_specs=out_specs,
        out_shape=out_shape,
        scratch_shapes=scratch,
        compiler_params=pltpu.CompilerParams(
            dimension_semantics=("arbitrary", "arbitrary"),
            vmem_limit_bytes=VMEM_LIMIT_LARGE_BYTES if sample else VMEM_LIMIT_BYTES),
        name="ffn_sample" if sample else "ffn_prompt",
    )(*args)
    if sample:
        return outs[0], outs[1], outs[2:]
    return outs[0], outs[1][tiles_per_seq - 1::tiles_per_seq], outs[2:]


def kernel(x_prompt, x_sample, state_conv_mix, state_conv_ffn, norm_mix_g, w_in, sgu_norm_g, w_s, b_s,
           conv_mix_w, conv_mix_b, out_norm_a_g, out_norm_b_g, w_out, norm_ffn_g, w_gate, w_up,
           conv_ffn_w, conv_ffn_b, w_down, final_norm_g):
    depth = w_in.shape[0]
    n_seq, seq, _ = x_prompt.shape
    n_dec, dec_seq, _ = x_sample.shape
    assert (seq, n_dec, dec_seq) == (2048, 128, 8)

    row3 = lambda a: a.reshape(depth, 1, a.shape[-1])
    tmajor = lambda a: jnp.swapaxes(a, -3, -2)
    p = {
        "norm_mix_g": row3(norm_mix_g), "sgu_norm_g": row3(sgu_norm_g),
        "out_norm_a_g": row3(out_norm_a_g), "out_norm_b_g": row3(out_norm_b_g),
        "norm_ffn_g": row3(norm_ffn_g),
        "conv_mix_wb": jnp.concatenate([conv_mix_w, conv_mix_b[:, None, :]], axis=1),
        "conv_ffn_wb": jnp.concatenate([conv_ffn_w, conv_ffn_b[:, None, :]], axis=1),
        "w_gate": w_gate, "w_up": w_up, "w_down": w_down,
        "w_s": w_s,
        "b_s_full": jnp.broadcast_to(b_s[..., None], (depth, N_HEADS, CHUNK, HEAD_DIM)),
        "w8": w_s[:, :, :dec_seq, :dec_seq].reshape(depth, -1),
        "b8": b_s[:, :, :dec_seq].reshape(depth, -1),
        "st_ffn": tmajor(state_conv_ffn),
        "final_norm_g": final_norm_g.reshape(1, D_MODEL),
    }
    st_mix = tmajor(state_conv_mix)

    xp = x_prompt.reshape(n_seq * seq, D_MODEL)
    xs = tmajor(x_sample)
    flat = lambda a: a.reshape(dec_seq * n_dec, D_MODEL)
    w_in_b, w_out_b = w_in[0].astype(BF16), w_out[0].astype(BF16)
    mix_p, ffn_p, mix_s, ffn_s, v_s = [], [], [], [], []
    for l in range(depth):
        final = l == depth - 1
        x1_s, h2_s, nm_s, v_rows = _mixer(xs, w_in_b, w_out_b, l, p, sample=True, st_in=st_mix)
        x2_s, nf_s, w_bf16 = _ffn(flat(h2_s), flat(x1_s), l, p, sample=True, final=final)
        x1_p, h2_p, nm_p = _mixer(xp, w_in_b, w_out_b, l, p, sample=False)
        xp, nf_p, proj_next = _ffn(h2_p, x1_p, l, p, sample=False, final=final, w_bf16=w_bf16,
                                   cast_next=None if final else (w_in, w_out))
        if not final:
            w_in_b, w_out_b = proj_next
        xs = x2_s.reshape(dec_seq, n_dec, D_MODEL)
        mix_p.append(nm_p)
        ffn_p.append(nf_p)
        mix_s.append(nm_s)
        ffn_s.append(nf_s)
        v_s.append(v_rows)
    y_prompt = xp.reshape(n_seq, seq, D_MODEL)
    y_sample = tmajor(xs)
    return (y_prompt, y_sample, jnp.stack(mix_p), jnp.stack(ffn_p),
            tmajor(jnp.stack(mix_s)), tmajor(jnp.stack(ffn_s)), tmajor(jnp.stack(v_s)))
```
